```python
import math
import jax, jax.numpy as jnp
from jax import lax
import numpy as np

D_MODEL = 2048
BATCH = 32
SEQ = 256
DEPTH = 2
DEC_BATCH = 2
DEC_SEQ = 2048
PAST_LEN = 256

GRID_W = 64
NA_HEADS = 8
NA_HD = 64
NA_KH = 8
NA_KW = 16
NA_QW = 16
NA_BAND = 32
NA_W = NA_HEADS * NA_HD
DA_HEADS = 4
DA_QK = 64
DA_VD = 2 * DA_QK
DA_QK_W = DA_HEADS * 2 * DA_QK
DA_V_W = DA_HEADS * DA_VD
SSM_CH = 512
SSM_P = 16
SSM_G = SSM_CH // SSM_P
SSM_N = 64
CONV_CH = 512
CONV_K = 31
SPLIT_SIZES = (NA_W, NA_W, NA_W, DA_QK_W, DA_QK_W, DA_V_W, SSM_CH, 2 * CONV_CH)
IN_W = 3 * NA_W + 2 * DA_QK_W + DA_V_W + SSM_CH + 2 * CONV_CH
N_BRANCH = 4
N_EXPERTS = 64
TOP_K = 8
EXPERT_F = 256
SHARED_F = 256
ROUTE_SCALE = 2.5
MOE_TOKEN_BLOCK = 1024
ROPE_BASE = 10000.0
EPS = 1e-6
NEG_INF = -1e30
Q_BLOCK = 128
DENSE_KEYS_LIMIT = 2048

kernel_name = 'hybrid_diffusion_prefix_trunk_step'


def _rmsnorm(x, g):
    xf = x.astype(jnp.float32)
    y = xf * lax.rsqrt(jnp.mean(xf * xf, axis=-1, keepdims=True) + EPS)
    return y.astype(x.dtype) * g


def _layernorm(x, g, b):
    xf = x.astype(jnp.float32)
    mu = jnp.mean(xf, axis=-1, keepdims=True)
    var = jnp.mean(jnp.square(xf - mu), axis=-1, keepdims=True)
    return ((xf - mu) * lax.rsqrt(var + EPS)).astype(x.dtype) * g + b


def _adaln(cvec, w, b):
    return (jax.nn.silu(cvec) @ w + b).reshape(cvec.shape[0], 6, D_MODEL)


def _mixer_inputs(x, mod, lp):
    h = _rmsnorm(x, lp['norm1_g']) * (1 + mod[:, None, 1]) + mod[:, None, 0]
    z = h @ lp['w_in']
    parts, start = [], 0
    for size in SPLIT_SIZES:
        parts.append(z[..., start:start + size])
        start += size
    return h, parts


def _over_query_blocks(fn, q, n_keys):
    B, L = q.shape[:2]
    if n_keys < DENSE_KEYS_LIMIT or L % Q_BLOCK:
        return fn(q)
    nb = L // Q_BLOCK
    qb = jnp.moveaxis(q.reshape(B, nb, Q_BLOCK, *q.shape[2:]), 1, 0)
    out = jnp.moveaxis(lax.map(fn, qb), 0, 1)
    return out.reshape(B, L, *out.shape[3:])


def _dense_attend(q, k, v):
    scale = q.shape[-1] ** -0.5

    def attend(qb):
        s = jnp.einsum('bqhd,bkhd->bhqk', qb, k).astype(jnp.float32) * scale
        p = jax.nn.softmax(s, axis=-1).astype(v.dtype)
        return jnp.einsum('bhqk,bkhd->bqhd', p, v)

    return _over_query_blocks(attend, q, k.shape[1])


def _natten_latent(q, k, v, ctx_k, ctx_v, rpb):
    B, L, H, D = q.shape
    rows = L // GRID_W
    kh = min(NA_KH, rows)
    ncb = GRID_W // NA_QW
    r = np.arange(rows)
    key_rows = np.clip(r - kh // 2, 0, rows - kh)[:, None] + np.arange(kh)
    jb = np.arange(ncb)
    key_cols = np.clip(jb * NA_QW - NA_KW // 2, 0, GRID_W - NA_BAND)[:, None] + np.arange(NA_BAND)
    q_cols = jb[:, None] * NA_QW + np.arange(NA_QW)
    win0 = np.clip(q_cols - NA_KW // 2, 0, GRID_W - NA_KW)[..., None]
    valid = (key_cols[:, None, :] >= win0) & (key_cols[:, None, :] < win0 + NA_KW)
    valid = np.broadcast_to(valid[:, :, None, :], (ncb, NA_QW, kh, NA_BAND)).reshape(ncb, NA_QW, kh * NA_BAND)
    d_row = key_rows - r[:, None] + (NA_KH - 1)
    d_col = np.clip(key_cols[:, None, :] - q_cols[..., None], 1 - NA_KW, NA_KW - 1) + (NA_KW - 1)
    bias = rpb[:, d_row[:, None, None, :, None], d_col[None, :, :, None, :]]
    bias = bias.reshape(H, rows, ncb, NA_QW, kh * NA_BAND).astype(jnp.float32)
    ridx = key_rows[:, None, :, None]
    cidx = key_cols[None, :, None, :]
    kg = k.reshape(B, rows, GRID_W, H, D)[:, ridx, cidx].reshape(B, rows, ncb, kh * NA_BAND, H, D)
    vg = v.reshape(B, rows, GRID_W, H, D)[:, ridx, cidx].reshape(B, rows, ncb, kh * NA_BAND, H, D)
    qg = q.reshape(B, rows, ncb, NA_QW, H, D)
    scale = D ** -0.5
    s_loc = jnp.einsum('brjqhd,brjnhd->bhrjqn', qg, kg).astype(jnp.float32) * scale + bias[None]
    s_loc = jnp.where(valid[None, None, None], s_loc, NEG_INF)
    s_ctx = jnp.einsum('brjqhd,bchd->bhrjqc', qg, ctx_k).astype(jnp.float32) * scale
    p = jax.nn.softmax(jnp.concatenate([s_loc, s_ctx], axis=-1), axis=-1).astype(v.dtype)
    n_loc = kh * NA_BAND
    o = (jnp.einsum('bhrjqn,brjnhd->brjqhd', p[..., :n_loc], vg)
         + jnp.einsum('bhrjqc,bchd->brjqhd', p[..., n_loc:], ctx_v))
    return o.reshape(B, L, H * D)


def _axial_rope_angles(L):
    half = DA_QK // 2
    inv = ROPE_BASE ** (-jnp.arange(0, half, 2, dtype=jnp.float32) / half)
    t = jnp.arange(L)
    pos = jnp.stack([t // GRID_W, t % GRID_W], axis=-1).astype(jnp.float32)
    return pos[:, :, None] * inv


def _rope(x, ang):
    B, L, H, M, E = x.shape
    xr = x.astype(jnp.float32).reshape(B, L, H, M, 2, 2, E // 4)
    cos = jnp.cos(ang)[None, :, None, None]
    sin = jnp.sin(ang)[None, :, None, None]
    x1, x2 = xr[..., 0, :], xr[..., 1, :]
    out = jnp.stack([x1 * cos - x2 * sin, x1 * sin + x2 * cos], axis=-2)
    return out.reshape(x.shape).astype(x.dtype)


def _diff_lambda(lam_q, lam_k, lam_init):
    lq = lam_q.astype(jnp.float32)
    lk = lam_k.astype(jnp.float32)
    return jnp.exp(jnp.sum(lq[0] * lk[0])) - jnp.exp(jnp.sum(lq[1] * lk[1])) + lam_init


def _diff_attend(q, k, v, lam):
    scale = DA_QK ** -0.5

    def attend(qb):
        s = jnp.einsum('bqhmd,bkhmd->bhmqk', qb, k).astype(jnp.float32) * scale
        p = jax.nn.softmax(s, axis=-1)
        a = (p[:, :, 0] - lam * p[:, :, 1]).astype(v.dtype)
        return jnp.einsum('bhqk,bkhe->bqhe', a, v)

    return _over_query_blocks(attend, q, k.shape[1])


def _diff_heads_out(o, subln_g, lam_init):
    B, L = o.shape[:2]
    return (_rmsnorm(o, subln_g) * (1.0 - lam_init)).reshape(B, L, DA_V_W)


def _s5_discretize(lam_re, lam_im, log_dt, b_re, b_im):
    dt = jnp.exp(log_dt.astype(jnp.float32))[:, None]
    lr = lam_re.astype(jnp.float32)
    li = lam_im.astype(jnp.float32)
    mag = jnp.exp(lr * dt)
    ar = mag * jnp.cos(li * dt)
    ai = mag * jnp.sin(li * dt)
    den = lr * lr + li * li
    fr = ((ar - 1.0) * lr + ai * li) / den
    fi = (ai * lr - (ar - 1.0) * li) / den
    br = b_re.astype(jnp.float32)
    bi = b_im.astype(jnp.float32)
    bbr = fr[..., None] * br - fi[..., None] * bi
    bbi = fr[..., None] * bi + fi[..., None] * br
    return ar, ai, bbr, bbi


def _ssm_combine(e1, e2):
    a1r, a1i, b1r, b1i = e1
    a2r, a2i, b2r, b2i = e2
    return (a1r * a2r - a1i * a2i,
            a1r * a2i + a1i * a2r,
            a2r * b1r - a2i * b1i + b2r,
            a2r * b1i + a2i * b1r + b2i)


def _s5_scan(uf, ar, ai, bbr, bbi, init, reverse):
    bur = jnp.einsum('blgp,gnp->blgn', uf, bbr)
    bui = jnp.einsum('blgp,gnp->blgn', uf, bbi)
    if init is not None:
        h0r, h0i = init
        pos = -1 if reverse else 0
        bur = bur.at[:, pos].add(ar * h0r - ai * h0i)
        bui = bui.at[:, pos].add(ar * h0i + ai * h0r)
    a_r = jnp.broadcast_to(ar, bur.shape)
    a_i = jnp.broadcast_to(ai, bur.shape)
    _, _, hr, hi = lax.associative_scan(_ssm_combine, (a_r, a_i, bur, bui), reverse=reverse, axis=1)
    return hr, hi


def _s5_mixer(u, lp, h0):
    B, L, _ = u.shape
    uf = u.astype(jnp.float32).reshape(B, L, SSM_G, SSM_P)
    y = uf * lp['ssm_d'].astype(jnp.float32).reshape(SSM_G, SSM_P)
    finals = []
    for direction in range(2):
        reverse = direction == 1
        ar, ai, bbr, bbi = _s5_discretize(lp['ssm_lam_re'][direction], lp['ssm_lam_im'][direction],
                                          lp['ssm_log_dt'][direction], lp['ssm_b_re'][direction],
                                          lp['ssm_b_im'][direction])
        init = None if h0 is None else (h0[:, direction, 0].astype(jnp.float32),
                                        h0[:, direction, 1].astype(jnp.float32))
        hr, hi = _s5_scan(uf, ar, ai, bbr, bbi, init, reverse)
        c_re = lp['ssm_c_re'][direction].astype(jnp.float32)
        c_im = lp['ssm_c_im'][direction].astype(jnp.float32)
        y = y + jnp.einsum('blgn,gpn->blgp', hr, c_re) - jnp.einsum('blgn,gpn->blgp', hi, c_im)
        last = 0 if reverse else L - 1
        finals.append(jnp.stack([hr[:, last], hi[:, last]], axis=1))
    y = jax.nn.gelu(y.reshape(B, L, SSM_CH)).astype(u.dtype)
    y = y * jax.nn.sigmoid(y @ lp['ssm_w_glu'])
    return y, jnp.stack(finals, axis=1).astype(u.dtype)


def _conv_module(cin, lp):
    z = cin[..., :CONV_CH] * jax.nn.sigmoid(cin[..., CONV_CH:])
    z = lax.conv_general_dilated(z, lp['conv_w'][:, None, :], window_strides=(1,), padding='SAME',
                                 dimension_numbers=('NWC', 'WIO', 'NWC'),
                                 feature_group_count=CONV_CH) + lp['conv_b']
    z = jax.nn.silu(_layernorm(z, lp['conv_ln_g'], lp['conv_ln_b']))
    return z @ lp['w_br_conv']


def _merge_branches(x, h, mod, branches, lp):
    B, L, _ = h.shape
    g = jax.nn.sigmoid(h @ lp['w_gate']).reshape(B, L, N_BRANCH, D_MODEL)
    y = g[:, :, 0] * branches[0] + g[:, :, 1] * branches[1] + g[:, :, 2] * branches[2] + g[:, :, 3] * branches[3]
    return x + mod[:, None, 2] * (y @ lp['w_out'])


def _moe(h, lp):
    B, L, _ = h.shape
    T = B * L
    ht = h.reshape(T, D_MODEL)
    scores = jax.nn.sigmoid((ht @ lp['moe_router']).astype(jnp.float32))
    _, idx = lax.top_k(scores + lp['moe_bias'].astype(jnp.float32), TOP_K)
    sel = jnp.take_along_axis(scores, idx, axis=-1)
    wts = sel / jnp.sum(sel, axis=-1, keepdims=True) * ROUTE_SCALE
    gates = jnp.einsum('tk,tke->te', wts, jax.nn.one_hot(idx, N_EXPERTS, dtype=jnp.float32))
    blk = math.gcd(T, MOE_TOKEN_BLOCK)
    w1, w3, w2 = lp['moe_w1'], lp['moe_w3'], lp['moe_w2']

    def routed(args):
        hb, gb = args
        a = jnp.einsum('td,edf->tef', hb, w1)
        b = jnp.einsum('td,edf->tef', hb, w3)
        z = jax.nn.silu(a) * b * gb[..., None].astype(hb.dtype)
        return jnp.einsum('tef,efd->td', z, w2)

    out = lax.map(routed, (ht.reshape(T // blk, blk, D_MODEL), gates.reshape(T // blk, blk, N_EXPERTS)))
    shared = (jax.nn.silu(ht @ lp['shared_w1']) * (ht @ lp['shared_w3'])) @ lp['shared_w2']
    return (out.reshape(T, D_MODEL) + shared).reshape(B, L, D_MODEL)


def _channel_sublayer(x, mod, lp):
    h = _rmsnorm(x, lp['norm2_g']) * (1 + mod[:, None, 4]) + mod[:, None, 3]
    return x + mod[:, None, 5] * _moe(h, lp)


def _context_layer(x, mod, lp, layer):
    B, L, _ = x.shape
    h, (naq, nak, nav, daq, dak, dav, u, cin) = _mixer_inputs(x, mod, lp)
    na_k = nak.reshape(B, L, NA_HEADS, NA_HD)
    na_v = nav.reshape(B, L, NA_HEADS, NA_HD)
    o_na = _dense_attend(naq.reshape(B, L, NA_HEADS, NA_HD), na_k, na_v).reshape(B, L, NA_W)
    da_k = dak.reshape(B, L, DA_HEADS, 2, DA_QK)
    da_v = dav.reshape(B, L, DA_HEADS, DA_VD)
    lam_init = 0.8 - 0.6 * math.exp(-0.3 * layer)
    lam = _diff_lambda(lp['da_lam_q'], lp['da_lam_k'], lam_init)
    o_da = _diff_heads_out(_diff_attend(daq.reshape(B, L, DA_HEADS, 2, DA_QK), da_k, da_v, lam),
                           lp['da_subln_g'], lam_init)
    o_ssm, ssm_state = _s5_mixer(u, lp, None)
    branches = (o_na @ lp['w_br_na'], o_da @ lp['w_br_da'], o_ssm @ lp['w_br_ssm'], _conv_module(cin, lp))
    x = _merge_branches(x, h, mod, branches, lp)
    x = _channel_sublayer(x, mod, lp)
    return x, (na_k, na_v, da_k, da_v, ssm_state)


def _latent_layer(x, mod, lp, layer, ctx_na_k, ctx_na_v, ctx_da_k, ctx_da_v, ctx_ssm):
    B, L, _ = x.shape
    h, (naq, nak, nav, daq, dak, dav, u, cin) = _mixer_inputs(x, mod, lp)
    o_na = _natten_latent(naq.reshape(B, L, NA_HEADS, NA_HD), nak.reshape(B, L, NA_HEADS, NA_HD),
                          nav.reshape(B, L, NA_HEADS, NA_HD), ctx_na_k, ctx_na_v, lp['na_rpb'])
    ang = _axial_rope_angles(L)
    q_da = _rope(daq.reshape(B, L, DA_HEADS, 2, DA_QK), ang)
    k_da = jnp.concatenate([_rope(dak.reshape(B, L, DA_HEADS, 2, DA_QK), ang), ctx_da_k], axis=1)
    v_da = jnp.concatenate([dav.reshape(B, L, DA_HEADS, DA_VD), ctx_da_v], axis=1)
    lam_init = 0.8 - 0.6 * math.exp(-0.3 * layer)
    lam = _diff_lambda(lp['da_lam_q'], lp['da_lam_k'], lam_init)
    o_da = _diff_heads_out(_diff_attend(q_da, k_da, v_da, lam), lp['da_subln_g'], lam_init)
    o_ssm, _ = _s5_mixer(u, lp, ctx_ssm)
    branches = (o_na @ lp['w_br_na'], o_da @ lp['w_br_da'], o_ssm @ lp['w_br_ssm'], _conv_module(cin, lp))
    x = _merge_branches(x, h, mod, branches, lp)
    return _channel_sublayer(x, mod, lp)


def setup_inputs(seed: int = 0) -> dict:
    key = jax.random.key(seed)
    ks = iter(jax.random.split(key, 64))
    f32 = jnp.float32

    def nrm(shape, scale=1.0):
        return jax.random.normal(next(ks), shape, f32) * scale

    def gain(shape):
        return 1.0 + nrm(shape, 0.02)

    d_in = D_MODEL ** -0.5
    lam_re = -0.5 + nrm((DEPTH, 2, SSM_G, SSM_N), 0.01)
    lam_im = math.pi * jnp.arange(SSM_N, dtype=f32) + nrm((DEPTH, 2, SSM_G, SSM_N), 0.01)
    log_dt = jax.random.uniform(next(ks), (DEPTH, 2, SSM_G), f32, math.log(1e-3), math.log(1e-1))
    return {
        'x_prompt': nrm((BATCH, SEQ, D_MODEL)),
        'x_sample': nrm((DEC_BATCH, DEC_SEQ, D_MODEL)),
        'cache_na_k': nrm((DEC_BATCH, DEPTH, PAST_LEN, NA_HEADS, NA_HD)),
        'cache_na_v': nrm((DEC_BATCH, DEPTH, PAST_LEN, NA_HEADS, NA_HD)),
        'cache_da_k': nrm((DEC_BATCH, DEPTH, PAST_LEN, DA_HEADS, 2, DA_QK)),
        'cache_da_v': nrm((DEC_BATCH, DEPTH, PAST_LEN, DA_HEADS, DA_VD)),
        'state_ssm': nrm((DEC_BATCH, DEPTH, 2, 2, SSM_G, SSM_N), 0.5),
        'c': nrm((DEC_BATCH, D_MODEL)),
        'c_ctx': nrm((D_MODEL,)),
        'ada_w': nrm((DEPTH, D_MODEL, 6 * D_MODEL), 0.5 * d_in),
        'ada_b': nrm((DEPTH, 6 * D_MODEL), 0.02),
        'norm1_g': gain((DEPTH, D_MODEL)),
        'norm2_g': gain((DEPTH, D_MODEL)),
        'w_in': nrm((DEPTH, D_MODEL, IN_W), d_in),
        'w_gate': nrm((DEPTH, D_MODEL, N_BRANCH * D_MODEL), d_in),
        'na_rpb': nrm((DEPTH, NA_HEADS, 2 * NA_KH - 1, 2 * NA_KW - 1), 0.02),
        'da_lam_q': nrm((DEPTH, 2, DA_QK), 0.1),
        'da_lam_k': nrm((DEPTH, 2, DA_QK), 0.1),
        'da_subln_g': gain((DEPTH, DA_VD)),
        'ssm_lam_re': lam_re,
        'ssm_lam_im': lam_im,
        'ssm_log_dt': log_dt,
        'ssm_b_re': nrm((DEPTH, 2, SSM_G, SSM_N, SSM_P), (2 * SSM_P) ** -0.5),
        'ssm_b_im': nrm((DEPTH, 2, SSM_G, SSM_N, SSM_P), (2 * SSM_P) ** -0.5),
        'ssm_c_re': nrm((DEPTH, 2, SSM_G, SSM_P, SSM_N), SSM_N ** -0.5),
        'ssm_c_im': nrm((DEPTH, 2, SSM_G, SSM_P, SSM_N), SSM_N ** -0.5),
        'ssm_d': nrm((DEPTH, SSM_CH)),
        'ssm_w_glu': nrm((DEPTH, SSM_CH, SSM_CH), SSM_CH ** -0.5),
        'conv_w': nrm((DEPTH, CONV_K, CONV_CH), CONV_K ** -0.5),
        'conv_b': nrm((DEPTH, CONV_CH), 0.02),
        'conv_ln_g': gain((DEPTH, CONV_CH)),
        'conv_ln_b': nrm((DEPTH, CONV_CH), 0.02),
        'w_br_na': nrm((DEPTH, NA_W, D_MODEL), NA_W ** -0.5),
        'w_br_da': nrm((DEPTH, DA_V_W, D_MODEL), DA_V_W ** -0.5),
        'w_br_ssm': nrm((DEPTH, SSM_CH, D_MODEL), SSM_CH ** -0.5),
        'w_br_conv': nrm((DEPTH, CONV_CH, D_MODEL), CONV_CH ** -0.5),
        'w_out': nrm((DEPTH, D_MODEL, D_MODEL), d_in),
        'moe_router': nrm((DEPTH, D_MODEL, N_EXPERTS), d_in),
        'moe_bias': nrm((DEPTH, N_EXPERTS), 0.01),
        'moe_w1': nrm((DEPTH, N_EXPERTS, D_MODEL, EXPERT_F), d_in),
        'moe_w3': nrm((DEPTH, N_EXPERTS, D_MODEL, EXPERT_F), d_in),
        'moe_w2': nrm((DEPTH, N_EXPERTS, EXPERT_F, D_MODEL), EXPERT_F ** -0.5),
        'shared_w1': nrm((DEPTH, D_MODEL, SHARED_F), d_in),
        'shared_w3': nrm((DEPTH, D_MODEL, SHARED_F), d_in),
        'shared_w2': nrm((DEPTH, SHARED_F, D_MODEL), SHARED_F ** -0.5),
        'final_g': gain((D_MODEL,)),
    }


def reference(x_prompt, x_sample, cache_na_k, cache_na_v, cache_da_k, cache_da_v, state_ssm, c, c_ctx,
              ada_w, ada_b, norm1_g, norm2_g, w_in, w_gate, na_rpb, da_lam_q, da_lam_k, da_subln_g,
              ssm_lam_re, ssm_lam_im, ssm_log_dt, ssm_b_re, ssm_b_im, ssm_c_re, ssm_c_im, ssm_d, ssm_w_glu,
              conv_w, conv_b, conv_ln_g, conv_ln_b, w_br_na, w_br_da, w_br_ssm, w_br_conv, w_out,
              moe_router, moe_bias, moe_w1, moe_w3, moe_w2, shared_w1, shared_w3, shared_w2, final_g):
    xp, xs = x_prompt, x_sample
    na_ks, na_vs, da_ks, da_vs, ssm_states = [], [], [], [], []
    for layer in range(DEPTH):
        lp = {
            'norm1_g': norm1_g[layer], 'norm2_g': norm2_g[layer], 'w_in': w_in[layer], 'w_gate': w_gate[layer],
            'na_rpb': na_rpb[layer], 'da_lam_q': da_lam_q[layer], 'da_lam_k': da_lam_k[layer],
            'da_subln_g': da_subln_g[layer], 'ssm_lam_re': ssm_lam_re[layer], 'ssm_lam_im': ssm_lam_im[layer],
            'ssm_log_dt': ssm_log_dt[layer], 'ssm_b_re': ssm_b_re[layer], 'ssm_b_im': ssm_b_im[layer],
            'ssm_c_re': ssm_c_re[layer], 'ssm_c_im': ssm_c_im[layer], 'ssm_d': ssm_d[layer],
            'ssm_w_glu': ssm_w_glu[layer], 'conv_w': conv_w[layer], 'conv_b': conv_b[layer],
            'conv_ln_g': conv_ln_g[layer], 'conv_ln_b': conv_ln_b[layer], 'w_br_na': w_br_na[layer],
            'w_br_da': w_br_da[layer], 'w_br_ssm': w_br_ssm[layer], 'w_br_conv': w_br_conv[layer],
            'w_out': w_out[layer], 'moe_router': moe_router[layer], 'moe_bias': moe_bias[layer],
            'moe_w1': moe_w1[layer], 'moe_w3': moe_w3[layer], 'moe_w2': moe_w2[layer],
            'shared_w1': shared_w1[layer], 'shared_w3': shared_w3[layer], 'shared_w2': shared_w2[layer],
        }
        mod_ctx = _adaln(c_ctx[None, :], ada_w[layer], ada_b[layer])
        mod_lat = _adaln(c, ada_w[layer], ada_b[layer])
        xp, (nk, nv, dk, dv, ss) = _context_layer(xp, mod_ctx, lp, layer)
        na_ks.append(nk)
        na_vs.append(nv)
        da_ks.append(dk)
        da_vs.append(dv)
        ssm_states.append(ss)
        xs = _latent_layer(xs, mod_lat, lp, layer, cache_na_k[:, layer], cache_na_v[:, layer],
                           cache_da_k[:, layer], cache_da_v[:, layer], state_ssm[:, layer])
    y_prompt = _rmsnorm(xp, final_g)
    y_sample = _rmsnorm(xs, final_g)
    ctx_na_k = jnp.stack(na_ks, axis=1)
    ctx_na_v = jnp.stack(na_vs, axis=1)
    ctx_da_k = jnp.stack(da_ks, axis=1)
    ctx_da_v = jnp.stack(da_vs, axis=1)
    ctx_ssm_state = jnp.stack(ssm_states, axis=1)
    return (y_prompt, y_sample, ctx_na_k, ctx_na_v, ctx_da_k, ctx_da_v, ctx_ssm_state)
```

```python
import functools
import math

import numpy as np
import jax
import jax.numpy as jnp
from jax import lax
from jax.experimental import pallas as pl
from jax.experimental.pallas import tpu as pltpu

F32 = jnp.float32
BF = jnp.bfloat16

D_MODEL = 2048
BATCH = 32
SEQ = 256
DEPTH = 2
DEC_BATCH = 2
DEC_SEQ = 2048
PAST_LEN = 256
GRID_W = 64
NA_HEADS = 8
NA_HD = 64
NA_KH = 8
NA_KW = 16
NA_W = NA_HEADS * NA_HD
DA_HEADS = 4
DA_QK = 64
DA_VD = 2 * DA_QK
SSM_CH = 512
SSM_P = 16
SSM_G = SSM_CH // SSM_P
SSM_N = 64
SSM_S = SSM_G * SSM_N
CONV_CH = 512
CONV_K = 31
IN_W = 4608
N_BRANCH = 4
N_EXPERTS = 64
TOP_K = 8
EXPERT_F = 256
ROUTE_SCALE = 2.5
ROPE_BASE = 10000.0
EPS = 1e-6
NEG_INF = -1e30

T_CTX = BATCH * SEQ
T_LAT = DEC_BATCH * DEC_SEQ
T_ALL = T_CTX + T_LAT

COL_NAQ, COL_NAK, COL_NAV, COL_DAQ, COL_DAK, COL_DAV, COL_U, COL_CA, COL_CG = range(9)

LANES = 128
ROWS_NA_Q = 4
ROWS_NA_K = 12
S5_ROWS = 512
VMEM_BIG = 56 * 1024 * 1024


def _params(n_axes, vmem=None):
    return pltpu.CompilerParams(dimension_semantics=("arbitrary",) * n_axes, vmem_limit_bytes=vmem)


def _silu(x):
    return x * jax.nn.sigmoid(x)


def _ada_kernel(c_ref, w_ref, b_ref, o_ref):
    c = c_ref[...]
    o_ref[0] = jnp.dot(_silu(c), w_ref[0], preferred_element_type=F32,
                       precision=lax.Precision.HIGHEST) + b_ref[0]


def _adaln(cvec, ada_w, ada_b):
    tn = 1536
    n = 6 * D_MODEL
    return pl.pallas_call(
        _ada_kernel,
        out_shape=jax.ShapeDtypeStruct((DEPTH, 8, n), F32),
        grid=(DEPTH, n // tn),
        in_specs=[pl.BlockSpec((8, D_MODEL), lambda l, j: (0, 0)),
                  pl.BlockSpec((1, D_MODEL, tn), lambda l, j: (l, 0, j)),
                  pl.BlockSpec((1, 1, tn), lambda l, j: (l, 0, j))],
        out_specs=pl.BlockSpec((1, 8, tn), lambda l, j: (l, 0, j)),
        compiler_params=_params(2, VMEM_BIG),
        name="adaln",
    )(cvec, ada_w, ada_b.reshape(DEPTH, 1, n))


def _mod_table(mods, tm):
    starts = np.arange(0, T_ALL, tm)
    idx = np.where(starts < T_CTX, 0, 1 + (starts - T_CTX) // DEC_SEQ)
    tab = mods[idx]
    return jnp.pad(tab, ((0, 0), (0, 2), (0, 0)))


def _in_kernel(x_ref, mod_ref, g_ref, w_ref, h_ref, z_ref):
    @pl.when(pl.program_id(1) == 0)
    def _():
        x = x_ref[...]
        y = x * lax.rsqrt(jnp.mean(x * x, axis=-1, keepdims=True) + EPS) * g_ref[...]
        h = y * (1.0 + mod_ref[0, 1:2, :]) + mod_ref[0, 0:1, :]
        h_ref[...] = h.astype(BF)

    z_ref[...] = jnp.dot(h_ref[...], w_ref[...], preferred_element_type=F32)


def _in_proj(x, modt, g, w):
    tm, tn = 512, 1536
    t = x.shape[0]
    return pl.pallas_call(
        _in_kernel,
        out_shape=(jax.ShapeDtypeStruct((t, D_MODEL), BF), jax.ShapeDtypeStruct((t, IN_W), F32)),
        grid=(t // tm, IN_W // tn),
        in_specs=[pl.BlockSpec((tm, D_MODEL), lambda i, j: (i, 0)),
                  pl.BlockSpec((1, 8, D_MODEL), lambda i, j: (i, 0, 0)),
                  pl.BlockSpec((1, D_MODEL), lambda i, j: (0, 0)),
                  pl.BlockSpec((D_MODEL, tn), lambda i, j: (0, j))],
        out_specs=(pl.BlockSpec((tm, D_MODEL), lambda i, j: (i, 0)),
                   pl.BlockSpec((tm, tn), lambda i, j: (i, j))),
        compiler_params=_params(2, VMEM_BIG),
        name="in_proj",
    )(x, modt, g.reshape(1, D_MODEL), w)


def _half_masks():
    lane = lax.broadcasted_iota(jnp.int32, (1, LANES), 1)
    lo = lane < (LANES // 2)
    return lo, jnp.logical_not(lo)


def _nt_dot(a, b):
    return lax.dot_general(a, b, (((1,), (1,)), ((), ())), preferred_element_type=F32)


def _softmax_parts(parts):
    m = jnp.max(parts[0], axis=-1, keepdims=True)
    for s in parts[1:]:
        m = jnp.maximum(m, jnp.max(s, axis=-1, keepdims=True))
    ps = [jnp.exp(s - m) for s in parts]
    den = jnp.sum(ps[0], axis=-1, keepdims=True)
    for p in ps[1:]:
        den = den + jnp.sum(p, axis=-1, keepdims=True)
    return ps, 1.0 / den


def _na_heads(q, k, v, o_ref, kc=None, vc=None, bias_ref=None):
    masks = _half_masks()
    for pair in range(NA_HEADS // 2):
        sl = slice(LANES * pair, LANES * (pair + 1))
        q128, k128, v128 = q[:, sl], k[:, sl], v[:, sl]
        o_pair = None
        for half in range(2):
            qm = jnp.where(masks[half], q128, 0.0).astype(BF)
            s = _nt_dot(qm, k128)
            if bias_ref is not None:
                s = s + bias_ref[0, 2 * pair + half]
            parts = [s]
            if kc is not None:
                parts.append(_nt_dot(qm, kc[:, sl]))
            ps, inv = _softmax_parts(parts)
            o = jnp.dot(ps[0].astype(BF), v128, preferred_element_type=F32)
            if kc is not None:
                o = o + jnp.dot(ps[1].astype(BF), vc[:, sl], preferred_element_type=F32)
            o = o * inv
            o_pair = o if half == 0 else jnp.where(masks[0], o_pair, o)
        o_ref[:, sl] = o_pair.astype(o_ref.dtype)


def _da_heads(q, k, v, lam, subg, out_scale, o_ref, kc=None, vc=None):
    masks = _half_masks()
    for h in range(DA_HEADS):
        sl = slice(LANES * h, LANES * (h + 1))
        q128, k128, v128 = q[:, sl], k[:, sl], v[:, sl]
        probs = []
        for m in range(2):
            qm = jnp.where(masks[m], q128, 0.0).astype(BF)
            parts = [_nt_dot(qm, k128)]
            if kc is not None:
                parts.append(_nt_dot(qm, kc[:, sl]))
            probs.append(_softmax_parts(parts))
        (p0, inv0), (p1, inv1) = probs
        r1 = lam * inv1
        a = (p0[0] * inv0 - p1[0] * r1).astype(BF)
        o = jnp.dot(a, v128, preferred_element_type=F32)
        if kc is not None:
            ac = (p0[1] * inv0 - p1[1] * r1).astype(BF)
            o = o + jnp.dot(ac, vc[:, sl], preferred_element_type=F32)
        o = o * lax.rsqrt(jnp.mean(o * o, axis=-1, keepdims=True) + EPS) * subg * out_scale
        o_ref[:, sl] = o.astype(o_ref.dtype)


def _attn_ctx_kernel(lam_ref, naq_ref, nak_ref, nav_ref, daq_ref, dak_ref, dav_ref, subg_ref,
                     ona_ref, oda_ref, *, out_scale):
    _na_heads(naq_ref[...] * (NA_HD ** -0.5), nak_ref[...].astype(BF), nav_ref[...].astype(BF), ona_ref)
    _da_heads(daq_ref[...] * (DA_QK ** -0.5), dak_ref[...].astype(BF), dav_ref[...].astype(BF),
              lam_ref[0, 0], subg_ref[...], out_scale, oda_ref)


def _attn_ctx(z, lam, subg, out_scale):
    def col(c):
        return pl.BlockSpec((SEQ, 512), lambda b: (b, c))

    return pl.pallas_call(
        functools.partial(_attn_ctx_kernel, out_scale=out_scale),
        out_shape=(jax.ShapeDtypeStruct((T_CTX, 512), BF), jax.ShapeDtypeStruct((T_CTX, 512), BF)),
        grid=(BATCH,),
        in_specs=[pl.BlockSpec(memory_space=pltpu.SMEM),
                  col(COL_NAQ), col(COL_NAK), col(COL_NAV), col(COL_DAQ), col(COL_DAK), col(COL_DAV),
                  pl.BlockSpec((1, LANES), lambda b: (0, 0))],
        out_specs=(pl.BlockSpec((SEQ, 512), lambda b: (b, 0)), pl.BlockSpec((SEQ, 512), lambda b: (b, 0))),
        compiler_params=_params(1, VMEM_BIG),
        name="attn_ctx",
    )(lam, z, z, z, z, z, z, subg)


def _rope_tables():
    half = DA_QK // 2
    inv = ROPE_BASE ** (-jnp.arange(0, half, 2, dtype=F32) / half)
    t = jnp.arange(DEC_SEQ)
    pos = jnp.stack([t // GRID_W, t % GRID_W], axis=-1).astype(F32)
    ang = pos[:, :, None] * inv
    ang64 = jnp.broadcast_to(ang[:, :, None, :], (DEC_SEQ, 2, 2, half // 2)).reshape(DEC_SEQ, DA_QK)
    sign = jnp.tile(jnp.repeat(jnp.array([-1.0, 1.0], F32), half // 2), 2)
    cos = jnp.tile(jnp.cos(ang64), (1, 512 // DA_QK))
    sin = jnp.tile(jnp.sin(ang64) * sign, (1, 512 // DA_QK))
    return cos, sin


def _rope_kernel(q_ref, k_ref, cos_ref, sin_ref, qo_ref, ko_ref):
    lane = lax.broadcasted_iota(jnp.int32, (1, 512), 1)
    first = (lane % (DA_QK // 2)) < (DA_QK // 4)
    cos, sin = cos_ref[...], sin_ref[...]

    def rope(x):
        partner = jnp.where(first, pltpu.roll(x, 512 - DA_QK // 4, 1), pltpu.roll(x, DA_QK // 4, 1))
        return x * cos + partner * sin

    qo_ref[...] = (rope(q_ref[...]) * (DA_QK ** -0.5)).astype(BF)
    ko_ref[...] = rope(k_ref[...]).astype(BF)


def _rope_lat(z, cos, sin):
    tm = 512
    base = T_CTX // tm
    per_seq = DEC_SEQ // tm
    return pl.pallas_call(
        _rope_kernel,
        out_shape=(jax.ShapeDtypeStruct((T_LAT, 512), BF), jax.ShapeDtypeStruct((T_LAT, 512), BF)),
        grid=(T_LAT // tm,),
        in_specs=[pl.BlockSpec((tm, 512), lambda i: (base + i, COL_DAQ)),
                  pl.BlockSpec((tm, 512), lambda i: (base + i, COL_DAK)),
                  pl.BlockSpec((tm, 512), lambda i: (i % per_seq, 0)),
                  pl.BlockSpec((tm, 512), lambda i: (i % per_seq, 0))],
        out_specs=(pl.BlockSpec((tm, 512), lambda i: (i, 0)), pl.BlockSpec((tm, 512), lambda i: (i, 0))),
        compiler_params=_params(1),
        name="rope_lat",
    )(z, z, cos, sin)


def _da_lat_kernel(lam_ref, q_ref, k_ref, v_ref, kc_ref, vc_ref, subg_ref, o_ref, *, out_scale):
    _da_heads(q_ref[...].astype(F32), k_ref[...], v_ref[...].astype(BF), lam_ref[0, 0], subg_ref[...],
              out_scale, o_ref, kc=kc_ref[0].astype(BF), vc=vc_ref[0].astype(BF))


def _da_lat(qr, kr, z, kc, vc, lam, subg, out_scale):
    tq = 256
    nq = DEC_SEQ // tq
    zb = T_CTX // DEC_SEQ
    return pl.pallas_call(
        functools.partial(_da_lat_kernel, out_scale=out_scale),
        out_shape=jax.ShapeDtypeStruct((T_LAT, 512), BF),
        grid=(DEC_BATCH, nq),
        in_specs=[pl.BlockSpec(memory_space=pltpu.SMEM),
                  pl.BlockSpec((tq, 512), lambda b, i: (b * nq + i, 0)),
                  pl.BlockSpec((DEC_SEQ, 512), lambda b, i: (b, 0)),
                  pl.BlockSpec((DEC_SEQ, 512), lambda b, i: (zb + b, COL_DAV)),
                  pl.BlockSpec((1, PAST_LEN, 512), lambda b, i: (b, 0, 0)),
                  pl.BlockSpec((1, PAST_LEN, 512), lambda b, i: (b, 0, 0)),
                  pl.BlockSpec((1, LANES), lambda b, i: (0, 0))],
        out_specs=pl.BlockSpec((tq, 512), lambda b, i: (b * nq + i, 0)),
        compiler_params=_params(2, VMEM_BIG),
        name="da_lat",
    )(lam, qr, kr, z, kc, vc, subg)


def _na_geometry():
    rows = DEC_SEQ // GRID_W
    variants = []
    for r_first, r0 in ((0, 0), (ROWS_NA_Q, 0), (rows - ROWS_NA_Q, rows - ROWS_NA_K)):
        qr = r_first + np.arange(ROWS_NA_Q)[:, None, None, None]
        qc = np.arange(GRID_W)[None, :, None, None]
        kr = r0 + np.arange(ROWS_NA_K)[None, None, :, None]
        kc = np.arange(GRID_W)[None, None, None, :]
        row0 = np.clip(qr - NA_KH // 2, 0, rows - NA_KH)
        win0 = np.clip(qc - NA_KW // 2, 0, GRID_W - NA_KW)
        valid = (kr >= row0) & (kr < row0 + NA_KH) & (kc >= win0) & (kc < win0 + NA_KW)
        d_row = np.clip(kr - qr + (NA_KH - 1), 0, 2 * NA_KH - 2)
        d_col = np.clip(kc - qc, 1 - NA_KW, NA_KW - 1) + (NA_KW - 1)
        shp = (ROWS_NA_Q * GRID_W, ROWS_NA_K * GRID_W)
        variants.append((np.broadcast_to(d_row, valid.shape).reshape(shp),
                         np.broadcast_to(d_col, valid.shape).reshape(shp), valid.reshape(shp)))
    d_row, d_col, valid = (np.stack(a) for a in zip(*variants))
    return d_row, d_col, valid


def _na_bias_table(rpb):
    d_row, d_col, valid = _na_geometry()
    bias = rpb[:, d_row, d_col]
    bias = jnp.where(valid[None], bias, NEG_INF)
    return jnp.transpose(bias, (1, 0, 2, 3)).astype(F32)


def _na_lat_kernel(q_ref, k_ref, v_ref, kc_ref, vc_ref, bias_ref, o_ref):
    g = pl.program_id(1)
    rows = DEC_SEQ // GRID_W
    r0 = jnp.clip(ROWS_NA_Q * g - NA_KH // 2, 0, rows - ROWS_NA_K)
    start = pl.multiple_of(r0 * GRID_W, GRID_W)
    nk = ROWS_NA_K * GRID_W
    k = k_ref[pl.ds(start, nk), :].astype(BF)
    v = v_ref[pl.ds(start, nk), :].astype(BF)
    _na_heads(q_ref[...] * (NA_HD ** -0.5), k, v, o_ref, kc=kc_ref[0].astype(BF), vc=vc_ref[0].astype(BF),
              bias_ref=bias_ref)


def _na_lat(z, kc, vc, bias):
    tq = ROWS_NA_Q * GRID_W
    ng = DEC_SEQ // tq
    nk = ROWS_NA_K * GRID_W
    qb = T_CTX // tq
    zb = T_CTX // DEC_SEQ

    def variant(g):
        return jnp.where(g == 0, 0, jnp.where(g == ng - 1, 2, 1))

    return pl.pallas_call(
        _na_lat_kernel,
        out_shape=jax.ShapeDtypeStruct((T_LAT, 512), BF),
        grid=(DEC_BATCH, ng),
        in_specs=[pl.BlockSpec((tq, 512), lambda b, g: (qb + b * ng + g, COL_NAQ)),
                  pl.BlockSpec((DEC_SEQ, 512), lambda b, g: (zb + b, COL_NAK)),
                  pl.BlockSpec((DEC_SEQ, 512), lambda b, g: (zb + b, COL_NAV)),
                  pl.BlockSpec((1, PAST_LEN, 512), lambda b, g: (b, 0, 0)),
                  pl.BlockSpec((1, PAST_LEN, 512), lambda b, g: (b, 0, 0)),
                  pl.BlockSpec((1, NA_HEADS, tq, nk), lambda b, g: (variant(g), 0, 0, 0))],
        out_specs=pl.BlockSpec((tq, 512), lambda b, g: (b * ng + g, 0)),
        compiler_params=_params(2, VMEM_BIG),
        name="na_lat",
    )(z, z, z, kc, vc, bias)


def _s5_discretize(lam_re, lam_im, log_dt, b_re, b_im):
    dt = jnp.exp(log_dt.astype(F32))[:, None]
    lr = lam_re.astype(F32)
    li = lam_im.astype(F32)
    mag = jnp.exp(lr * dt)
    ar = mag * jnp.cos(li * dt)
    ai = mag * jnp.sin(li * dt)
    den = lr * lr + li * li
    fr = ((ar - 1.0) * lr + ai * li) / den
    fi = (ai * lr - (ar - 1.0) * li) / den
    br = b_re.astype(F32)
    bi = b_im.astype(F32)
    bbr = fr[..., None] * br - fi[..., None] * bi
    bbi = fr[..., None] * bi + fi[..., None] * br
    return ar, ai, bbr, bbi


def _s5_weights(ar, ai, bbr, bbi, c_re, c_im):
    eye = jnp.eye(8, dtype=F32)
    nblk = SSM_G // 8

    def in_w(bb):
        return jnp.einsum('jgnp,gh->jgphn', bb.reshape(nblk, 8, SSM_N, SSM_P), eye).reshape(nblk, LANES, 8 * SSM_N)

    def out_w(c):
        return jnp.einsum('jgpn,gh->jgnhp', c.astype(F32).reshape(nblk, 8, SSM_P, SSM_N), eye).reshape(
            nblk, 8 * SSM_N, LANES)

    wb = jnp.concatenate([in_w(bbr), in_w(bbi)], axis=-1).astype(BF)
    wc = jnp.concatenate([out_w(c_re), -out_w(c_im)], axis=1).astype(BF)
    a_row = jnp.concatenate([ar.reshape(1, SSM_S), ai.reshape(1, SSM_S)], axis=1)
    return wb, wc, a_row


def _s5_kernel(uf_ref, ub_ref, wbf_ref, wbb_ref, wcf_ref, wcb_ref, af_ref, ab_ref, h0f_ref, h0b_ref,
               yf_ref, yb_ref, hff_ref, hfb_ref, hf_scr, hb_scr, *, nc, tc):
    r = nc * tc
    nblk = SSM_G // 8
    sb = 8 * SSM_N

    @pl.when(pl.program_id(0) == 0)
    def _():
        hf_scr[0:nc, :] = h0f_ref[...]
        hb_scr[r:r + nc, :] = h0b_ref[...]

    for j in range(nblk):
        cs = slice(LANES * j, LANES * (j + 1))
        pf = jnp.dot(uf_ref[:, cs], wbf_ref[j], preferred_element_type=F32)
        hf_scr[nc:nc + r, sb * j:sb * (j + 1)] = pf[:, :sb]
        hf_scr[nc:nc + r, SSM_S + sb * j:SSM_S + sb * (j + 1)] = pf[:, sb:]
        pb = jnp.dot(ub_ref[:, cs], wbb_ref[j], preferred_element_type=F32)
        hb_scr[0:r, sb * j:sb * (j + 1)] = pb[:, :sb]
        hb_scr[0:r, SSM_S + sb * j:SSM_S + sb * (j + 1)] = pb[:, sb:]

    def step(t, carry):
        f_prev = pl.multiple_of(t * nc, nc)
        f_cur = pl.multiple_of((t + 1) * nc, nc)
        b_cur = pl.multiple_of((tc - 1 - t) * nc, nc)
        b_prev = pl.multiple_of((tc - t) * nc, nc)
        for scr, a_ref, prev, cur in ((hf_scr, af_ref, f_prev, f_cur), (hb_scr, ab_ref, b_prev, b_cur)):
            for lb in range(nblk):
                lr = slice(sb * lb, sb * (lb + 1))
                li = slice(SSM_S + sb * lb, SSM_S + sb * (lb + 1))
                hr, hi = scr[pl.ds(prev, nc), lr], scr[pl.ds(prev, nc), li]
                ar, ai = a_ref[:, lr], a_ref[:, li]
                scr[pl.ds(cur, nc), lr] = ar * hr - ai * hi + scr[pl.ds(cur, nc), lr]
                scr[pl.ds(cur, nc), li] = ar * hi + ai * hr + scr[pl.ds(cur, nc), li]
        return carry

    lax.fori_loop(0, tc, step, 0)

    for j in range(nblk):
        cs = slice(LANES * j, LANES * (j + 1))
        lr = slice(sb * j, sb * (j + 1))
        li = slice(SSM_S + sb * j, SSM_S + sb * (j + 1))
        yf_ref[:, cs] = (jnp.dot(hf_scr[nc:nc + r, lr].astype(BF), wcf_ref[j, :sb], preferred_element_type=F32)
                         + jnp.dot(hf_scr[nc:nc + r, li].astype(BF), wcf_ref[j, sb:], preferred_element_type=F32))
        yb_ref[:, cs] = (jnp.dot(hb_scr[0:r, lr].astype(BF), wcb_ref[j, :sb], preferred_element_type=F32)
                         + jnp.dot(hb_scr[0:r, li].astype(BF), wcb_ref[j, sb:], preferred_element_type=F32))

    last_f = hf_scr[r:r + nc, :]
    last_b = hb_scr[0:nc, :]
    hf_scr[0:nc, :] = last_f
    hb_scr[r:r + nc, :] = last_b
    hff_ref[...] = last_f
    hfb_ref[...] = last_b


def _s5_scan(u_tm, nc, wts_f, wts_b, h0f, h0b):
    rows = u_tm.shape[0]
    tc = S5_ROWS // nc
    n_chunks = rows // S5_ROWS
    wbf, wcf, af = wts_f
    wbb, wcb, ab = wts_b
    af = jnp.broadcast_to(af, (nc, 2 * SSM_S))
    ab = jnp.broadcast_to(ab, (nc, 2 * SSM_S))
    const3 = lambda c: (0, 0, 0)
    const2 = lambda c: (0, 0)
    return pl.pallas_call(
        functools.partial(_s5_kernel, nc=nc, tc=tc),
        out_shape=(jax.ShapeDtypeStruct((rows, SSM_CH), F32), jax.ShapeDtypeStruct((rows, SSM_CH), F32),
                   jax.ShapeDtypeStruct((nc, 2 * SSM_S), F32), jax.ShapeDtypeStruct((nc, 2 * SSM_S), F32)),
        grid=(n_chunks,),
        in_specs=[pl.BlockSpec((S5_ROWS, SSM_CH), lambda c: (c, 0)),
                  pl.BlockSpec((S5_ROWS, SSM_CH), lambda c: (n_chunks - 1 - c, 0)),
                  pl.BlockSpec(wbf.shape, const3), pl.BlockSpec(wbb.shape, const3),
                  pl.BlockSpec(wcf.shape, const3), pl.BlockSpec(wcb.shape, const3),
                  pl.BlockSpec((nc, 2 * SSM_S), const2), pl.BlockSpec((nc, 2 * SSM_S), const2),
                  pl.BlockSpec((nc, 2 * SSM_S), const2), pl.BlockSpec((nc, 2 * SSM_S), const2)],
        out_specs=(pl.BlockSpec((S5_ROWS, SSM_CH), lambda c: (c, 0)),
                   pl.BlockSpec((S5_ROWS, SSM_CH), lambda c: (n_chunks - 1 - c, 0)),
                   pl.BlockSpec((nc, 2 * SSM_S), const2), pl.BlockSpec((nc, 2 * SSM_S), const2)),
        scratch_shapes=[pltpu.VMEM((S5_ROWS + nc, 2 * SSM_S), F32), pltpu.VMEM((S5_ROWS + nc, 2 * SSM_S), F32)],
        compiler_params=_params(1, VMEM_BIG),
        name="s5_scan",
    )(u_tm, u_tm, wbf, wbb, wcf, wcb, af, ab, h0f, h0b)


def _ssm_glu_kernel(u_ref, y_ref, d_ref, w_ref, o_ref):
    y = jax.nn.gelu(u_ref[...] * d_ref[...] + y_ref[...])
    o_ref[...] = (y * jax.nn.sigmoid(jnp.dot(y.astype(BF), w_ref[...], preferred_element_type=F32))).astype(BF)


def _ssm_glu(z, y_scan, d, w_glu):
    tm = 1024
    return pl.pallas_call(
        _ssm_glu_kernel,
        out_shape=jax.ShapeDtypeStruct((T_ALL, SSM_CH), BF),
        grid=(T_ALL // tm,),
        in_specs=[pl.BlockSpec((tm, SSM_CH), lambda i: (i, COL_U)),
                  pl.BlockSpec((tm, SSM_CH), lambda i: (i, 0)),
                  pl.BlockSpec((1, SSM_CH), lambda i: (0, 0)),
                  pl.BlockSpec((SSM_CH, SSM_CH), lambda i: (0, 0))],
        out_specs=pl.BlockSpec((tm, SSM_CH), lambda i: (i, 0)),
        compiler_params=_params(1),
        name="ssm_glu",
    )(z, y_scan, d.reshape(1, SSM_CH), w_glu)


CONV_PAD = 16
CONV_ROWS = 64


def _conv_kernel(a_ref, g_ref, w_ref, b_ref, lg_ref, lb_ref, o_ref, z_scr, *, seq):
    z_scr[0:CONV_PAD, :] = jnp.zeros((CONV_PAD, CONV_CH), F32)
    z_scr[CONV_PAD + seq:2 * CONV_PAD + seq, :] = jnp.zeros((CONV_PAD, CONV_CH), F32)
    z_scr[CONV_PAD:CONV_PAD + seq, :] = a_ref[...] * jax.nn.sigmoid(g_ref[...])
    off = CONV_PAD - CONV_K // 2

    def tile(i, carry):
        base = pl.multiple_of(i * CONV_ROWS, CONV_ROWS)
        win = z_scr[pl.ds(base, CONV_ROWS + 2 * CONV_PAD), :]
        acc = jnp.zeros((CONV_ROWS, CONV_CH), F32)
        for k in range(CONV_K):
            acc = acc + win[off + k:off + k + CONV_ROWS, :] * w_ref[k:k + 1, :]
        acc = acc + b_ref[...]
        mu = jnp.mean(acc, axis=-1, keepdims=True)
        cen = acc - mu
        var = jnp.mean(cen * cen, axis=-1, keepdims=True)
        y = cen * lax.rsqrt(var + EPS) * lg_ref[...] + lb_ref[...]
        o_ref[pl.ds(base, CONV_ROWS), :] = _silu(y).astype(BF)
        return carry

    lax.fori_loop(0, seq // CONV_ROWS, tile, 0)


def _conv_module(z, seq, row_block0, n_seq, w, b, lg, lb):
    row = lambda v: v.reshape(1, CONV_CH)
    wpad = jnp.pad(w, ((0, 32 - CONV_K), (0, 0)))
    vec = pl.BlockSpec((1, CONV_CH), lambda s: (0, 0))
    return pl.pallas_call(
        functools.partial(_conv_kernel, seq=seq),
        out_shape=jax.ShapeDtypeStruct((n_seq * seq, CONV_CH), BF),
        grid=(n_seq,),
        in_specs=[pl.BlockSpec((seq, CONV_CH), lambda s: (row_block0 + s, COL_CA)),
                  pl.BlockSpec((seq, CONV_CH), lambda s: (row_block0 + s, COL_CG)),
                  pl.BlockSpec((32, CONV_CH), lambda s: (0, 0)), vec, vec, vec],
        out_specs=pl.BlockSpec((seq, CONV_CH), lambda s: (s, 0)),
        scratch_shapes=[pltpu.VMEM((seq + 2 * CONV_PAD, CONV_CH), F32)],
        compiler_params=_params(1, VMEM_BIG),
        name="conv_module",
    )(z, z, wpad, row(b), row(lg), row(lb))


def _merge_kernel(h_ref, o_ref, wg_ref, wb_ref, y_ref, acc_ref):
    j = pl.program_id(2)
    gate = jax.nn.sigmoid(jnp.dot(h_ref[...], wg_ref[...], preferred_element_type=F32))
    t = gate * jnp.dot(o_ref[0], wb_ref[0], preferred_element_type=F32)

    @pl.when(j == 0)
    def _():
        acc_ref[...] = t

    @pl.when(j > 0)
    def _():
        acc_ref[...] += t

    @pl.when(j == N_BRANCH - 1)
    def _():
        y_ref[...] = acc_ref[...].astype(BF)


def _merge(h, branches, w_gate, w_br):
    tm, tn = 512, 1024
    nct = D_MODEL // tn
    return pl.pallas_call(
        _merge_kernel,
        out_shape=jax.ShapeDtypeStruct((T_ALL, D_MODEL), BF),
        grid=(T_ALL // tm, nct, N_BRANCH),
        in_specs=[pl.BlockSpec((tm, D_MODEL), lambda i, c, j: (i, 0)),
                  pl.BlockSpec((1, tm, 512), lambda i, c, j: (j, i, 0)),
                  pl.BlockSpec((D_MODEL, tn), lambda i, c, j: (0, j * nct + c)),
                  pl.BlockSpec((1, 512, tn), lambda i, c, j: (j, 0, c))],
        out_specs=pl.BlockSpec((tm, tn), lambda i, c, j: (i, c)),
        scratch_shapes=[pltpu.VMEM((tm, tn), F32)],
        compiler_params=_params(3, VMEM_BIG),
        name="merge",
    )(h, branches, w_gate, w_br)


def _out_router_kernel(y_ref, w_ref, x_ref, mod_ref, g_ref, rh_ref, rl_ref, rb_ref, xo_ref, h_ref, gate_ref):
    x = x_ref[...] + mod_ref[0, 2:3, :] * jnp.dot(y_ref[...], w_ref[...], preferred_element_type=F32)
    xo_ref[...] = x
    n = x * lax.rsqrt(jnp.mean(x * x, axis=-1, keepdims=True) + EPS) * g_ref[...]
    h = n * (1.0 + mod_ref[0, 4:5, :]) + mod_ref[0, 3:4, :]
    hb = h.astype(BF)
    h_ref[...] = hb
    h_lo = (h - hb.astype(F32)).astype(BF)
    logits = (jnp.dot(hb, rh_ref[...], preferred_element_type=F32)
              + jnp.dot(h_lo, rh_ref[...], preferred_element_type=F32)
              + jnp.dot(hb, rl_ref[...], preferred_element_type=F32))
    scores = jax.nn.sigmoid(logits)
    lane = lax.broadcasted_iota(jnp.int32, scores.shape, 1)
    cand = jnp.where(lane < N_EXPERTS, scores + rb_ref[...], -jnp.inf)
    picked = jnp.zeros(scores.shape, jnp.bool_)
    for _ in range(TOP_K):
        m = jnp.max(cand, axis=-1, keepdims=True)
        first = jnp.min(jnp.where(cand == m, lane, LANES), axis=-1, keepdims=True)
        hit = lane == first
        picked = jnp.logical_or(picked, hit)
        cand = jnp.where(hit, -jnp.inf, cand)
    sel = jnp.where(picked, scores, 0.0)
    gates = sel / jnp.sum(sel, axis=-1, keepdims=True) * ROUTE_SCALE
    gate_ref[...] = jnp.where(lane == N_EXPERTS, 1.0, gates)


def _out_router(y, w_out, x, modt, g2, r_hi, r_lo, r_bias):
    tm = 512
    full = lambda i: (0, 0)
    return pl.pallas_call(
        _out_router_kernel,
        out_shape=(jax.ShapeDtypeStruct((T_ALL, D_MODEL), F32), jax.ShapeDtypeStruct((T_ALL, D_MODEL), BF),
                   jax.ShapeDtypeStruct((T_ALL, LANES), F32)),
        grid=(T_ALL // tm,),
        in_specs=[pl.BlockSpec((tm, D_MODEL), lambda i: (i, 0)),
                  pl.BlockSpec((D_MODEL, D_MODEL), full),
                  pl.BlockSpec((tm, D_MODEL), lambda i: (i, 0)),
                  pl.BlockSpec((1, 8, D_MODEL), lambda i: (i, 0, 0)),
                  pl.BlockSpec((1, D_MODEL), full),
                  pl.BlockSpec((D_MODEL, LANES), full), pl.BlockSpec((D_MODEL, LANES), full),
                  pl.BlockSpec((1, LANES), full)],
        out_specs=(pl.BlockSpec((tm, D_MODEL), lambda i: (i, 0)), pl.BlockSpec((tm, D_MODEL), lambda i: (i, 0)),
                   pl.BlockSpec((tm, LANES), lambda i: (i, 0))),
        compiler_params=_params(1, VMEM_BIG),
        name="out_router",
    )(y, w_out, x, modt, g2.reshape(1, D_MODEL), r_hi, r_lo, r_bias)


def _moe_kernel(h_ref, gate_ref, w13_ref, w2_ref, x_ref, mod_ref, o_ref, acc_ref):
    e = pl.program_id(1)
    ab = jnp.dot(h_ref[...], w13_ref[0], preferred_element_type=F32)
    a, b = ab[:, :EXPERT_F], ab[:, EXPERT_F:]
    lane = lax.broadcasted_iota(jnp.int32, gate_ref.shape, 1)
    gcol = jnp.sum(jnp.where(lane == e, gate_ref[...], 0.0), axis=-1, keepdims=True)
    zed = (_silu(a) * b * gcol).astype(BF)
    t = jnp.dot(zed, w2_ref[0], preferred_element_type=F32)

    @pl.when(e == 0)
    def _():
        acc_ref[...] = t

    @pl.when(e > 0)
    def _():
        acc_ref[...] += t

    @pl.when(e == pl.num_programs(1) - 1)
    def _():
        o_ref[...] = x_ref[...] + mod_ref[0, 5:6, :] * acc_ref[...]


def _moe(h, gates, w13, w2, x, modt):
    tm = 512
    ne = w13.shape[0]
    return pl.pallas_call(
        _moe_kernel,
        out_shape=jax.ShapeDtypeStruct((T_ALL, D_MODEL), F32),
        grid=(T_ALL // tm, ne),
        in_specs=[pl.BlockSpec((tm, D_MODEL), lambda i, e: (i, 0)),
                  pl.BlockSpec((tm, LANES), lambda i, e: (i, 0)),
                  pl.BlockSpec((1, D_MODEL, 2 * EXPERT_F), lambda i, e: (e, 0, 0)),
                  pl.BlockSpec((1, EXPERT_F, D_MODEL), lambda i, e: (e, 0, 0)),
                  pl.BlockSpec((tm, D_MODEL), lambda i, e: (i, 0)),
                  pl.BlockSpec((1, 8, D_MODEL), lambda i, e: (i, 0, 0))],
        out_specs=pl.BlockSpec((tm, D_MODEL), lambda i, e: (i, 0)),
        scratch_shapes=[pltpu.VMEM((tm, D_MODEL), F32)],
        compiler_params=_params(2, VMEM_BIG),
        name="moe",
    )(h, gates, w13, w2, x, modt)


def _final_kernel(x_ref, g_ref, o_ref):
    x = x_ref[...]
    o_ref[...] = x * lax.rsqrt(jnp.mean(x * x, axis=-1, keepdims=True) + EPS) * g_ref[...]


def _final_norm(x, g):
    tm = 512
    return pl.pallas_call(
        _final_kernel,
        out_shape=jax.ShapeDtypeStruct((T_ALL, D_MODEL), F32),
        grid=(T_ALL // tm,),
        in_specs=[pl.BlockSpec((tm, D_MODEL), lambda i: (i, 0)), pl.BlockSpec((1, D_MODEL), lambda i: (0, 0))],
        out_specs=pl.BlockSpec((tm, D_MODEL), lambda i: (i, 0)),
        compiler_params=_params(1),
        name="final_norm",
    )(x, g.reshape(1, D_MODEL))


def _time_major(u, n_chain):
    b, l, c = u.shape
    u = jnp.transpose(u, (1, 0, 2))
    u = jnp.pad(u, ((0, 0), (0, n_chain - b), (0, 0)))
    return u.reshape(l * n_chain, c)


def _from_time_major(y, b, l, n_chain):
    return jnp.transpose(y.reshape(l, n_chain, -1)[:, :b], (1, 0, 2)).reshape(b * l, -1)


def _diff_lambda(lam_q, lam_k, lam_init):
    lq = lam_q.astype(F32)
    lk = lam_k.astype(F32)
    return jnp.exp(jnp.sum(lq[0] * lk[0])) - jnp.exp(jnp.sum(lq[1] * lk[1])) + lam_init


def kernel(x_prompt, x_sample, cache_na_k, cache_na_v, cache_da_k, cache_da_v, state_ssm, c, c_ctx, ada_w, ada_b, norm1_g, norm2_g, w_in, w_gate, na_rpb, da_lam_q, da_lam_k, da_subln_g, ssm_lam_re, ssm_lam_im, ssm_log_dt, ssm_b_re, ssm_b_im, ssm_c_re, ssm_c_im, ssm_d, ssm_w_glu, conv_w, conv_b, conv_ln_g, conv_ln_b, w_br_na, w_br_da, w_br_ssm, w_br_conv, w_out, moe_router, moe_bias, moe_w1, moe_w3, moe_w2, shared_w1, shared_w3, shared_w2, final_g):
    x = jnp.concatenate([x_prompt.reshape(T_CTX, D_MODEL), x_sample.reshape(T_LAT, D_MODEL)], axis=0)
    cvec = jnp.zeros((8, D_MODEL), F32).at[0].set(c_ctx).at[1:1 + DEC_BATCH].set(c)
    mods_all = _adaln(cvec, ada_w, ada_b)
    cos, sin = _rope_tables()
    lat_chains = 8

    na_ks, na_vs, da_ks, da_vs, ssm_states = [], [], [], [], []
    for layer in range(DEPTH):
        mods = mods_all[layer, :1 + DEC_BATCH].reshape(1 + DEC_BATCH, 6, D_MODEL)
        modt = _mod_table(mods, 512)
        h1, z = _in_proj(x, modt, norm1_g[layer], w_in[layer].astype(BF))

        zc = z[:T_CTX]
        na_ks.append(zc[:, 512 * COL_NAK:512 * (COL_NAK + 1)].reshape(BATCH, SEQ, NA_HEADS, NA_HD))
        na_vs.append(zc[:, 512 * COL_NAV:512 * (COL_NAV + 1)].reshape(BATCH, SEQ, NA_HEADS, NA_HD))
        da_ks.append(zc[:, 512 * COL_DAK:512 * (COL_DAK + 1)].reshape(BATCH, SEQ, DA_HEADS, 2, DA_QK))
        da_vs.append(zc[:, 512 * COL_DAV:512 * (COL_DAV + 1)].reshape(BATCH, SEQ, DA_HEADS, DA_VD))

        lam_init = 0.8 - 0.6 * math.exp(-0.3 * layer)
        lam = _diff_lambda(da_lam_q[layer], da_lam_k[layer], lam_init).reshape(1, 1)
        subg = da_subln_g[layer].reshape(1, DA_VD)
        ona_c, oda_c = _attn_ctx(z, lam, subg, 1.0 - lam_init)
        ona_l = _na_lat(z, cache_na_k[:, layer].reshape(DEC_BATCH, PAST_LEN, 512),
                        cache_na_v[:, layer].reshape(DEC_BATCH, PAST_LEN, 512), _na_bias_table(na_rpb[layer]))
        qr, kr = _rope_lat(z, cos, sin)
        oda_l = _da_lat(qr, kr, z, cache_da_k[:, layer].reshape(DEC_BATCH, PAST_LEN, 512),
                        cache_da_v[:, layer].reshape(DEC_BATCH, PAST_LEN, 512), lam, subg, 1.0 - lam_init)

        wts = []
        for d in range(2):
            ar, ai, bbr, bbi = _s5_discretize(ssm_lam_re[layer, d], ssm_lam_im[layer, d], ssm_log_dt[layer, d],
                                              ssm_b_re[layer, d], ssm_b_im[layer, d])
            wts.append(_s5_weights(ar, ai, bbr, bbi, ssm_c_re[layer, d], ssm_c_im[layer, d]))
        u = z[:, 512 * COL_U:512 * (COL_U + 1)].astype(BF)
        zeros_c = jnp.zeros((BATCH, 2 * SSM_S), F32)
        yf, yb, hf, hb = _s5_scan(_time_major(u[:T_CTX].reshape(BATCH, SEQ, SSM_CH), BATCH), BATCH,
                                  wts[0], wts[1], zeros_c, zeros_c)
        y_ctx = _from_time_major(yf + yb, BATCH, SEQ, BATCH)
        ssm_states.append(jnp.stack([hf.reshape(BATCH, 2, SSM_G, SSM_N), hb.reshape(BATCH, 2, SSM_G, SSM_N)], axis=1))
        h0 = state_ssm[:, layer].astype(F32).reshape(DEC_BATCH, 2, 2 * SSM_S)
        h0 = jnp.pad(h0, ((0, lat_chains - DEC_BATCH), (0, 0), (0, 0)))
        yf, yb, _, _ = _s5_scan(_time_major(u[T_CTX:].reshape(DEC_BATCH, DEC_SEQ, SSM_CH), lat_chains), lat_chains,
                                wts[0], wts[1], h0[:, 0], h0[:, 1])
        y_lat = _from_time_major(yf + yb, DEC_BATCH, DEC_SEQ, lat_chains)
        o_ssm = _ssm_glu(z, jnp.concatenate([y_ctx, y_lat], axis=0), ssm_d[layer], ssm_w_glu[layer].astype(BF))

        conv_args = (conv_w[layer], conv_b[layer], conv_ln_g[layer], conv_ln_b[layer])
        o_conv = jnp.concatenate([_conv_module(z, SEQ, 0, BATCH, *conv_args),
                                  _conv_module(z, DEC_SEQ, T_CTX // DEC_SEQ, DEC_BATCH, *conv_args)], axis=0)

        branches = jnp.stack([jnp.concatenate([ona_c, ona_l], axis=0), jnp.concatenate([oda_c, oda_l], axis=0),
                              o_ssm, o_conv], axis=0)
        w_br = jnp.stack([w_br_na[layer], w_br_da[layer], w_br_ssm[layer], w_br_conv[layer]], axis=0).astype(BF)
        y = _merge(h1, branches, w_gate[layer].astype(BF), w_br)

        router = jnp.pad(moe_router[layer], ((0, 0), (0, LANES - N_EXPERTS)))
        r_hi = router.astype(BF)
        r_lo = (router - r_hi.astype(F32)).astype(BF)
        r_bias = jnp.pad(moe_bias[layer].astype(F32), (0, LANES - N_EXPERTS)).reshape(1, LANES)
        x_mid, h2, gates = _out_router(y, w_out[layer].astype(BF), x, modt, norm2_g[layer], r_hi, r_lo, r_bias)
        w13 = jnp.concatenate([jnp.concatenate([moe_w1[layer], moe_w3[layer]], axis=-1),
                               jnp.concatenate([shared_w1[layer], shared_w3[layer]], axis=-1)[None]], axis=0).astype(BF)
        w2 = jnp.concatenate([moe_w2[layer], shared_w2[layer][None]], axis=0).astype(BF)
        x = _moe(h2, gates, w13, w2, x_mid, modt)

    y = _final_norm(x, final_g)
    y_prompt = y[:T_CTX].reshape(BATCH, SEQ, D_MODEL)
    y_sample = y[T_CTX:].reshape(DEC_BATCH, DEC_SEQ, D_MODEL)
    return (y_prompt, y_sample, jnp.stack(na_ks, axis=1), jnp.stack(na_vs, axis=1), jnp.stack(da_ks, axis=1),
            jnp.stack(da_vs, axis=1), jnp.stack(ssm_states, axis=1))
```

```python
import functools
import math

import numpy as np
import jax
import jax.numpy as jnp
from jax import lax
from jax.experimental import pallas as pl
from jax.experimental.pallas import tpu as pltpu

F32 = jnp.float32
BF = jnp.bfloat16

D_MODEL = 2048
BATCH = 32
SEQ = 256
DEPTH = 2
DEC_BATCH = 2
DEC_SEQ = 2048
PAST_LEN = 256
GRID_W = 64
NA_HEADS = 8
NA_HD = 64
NA_KH = 8
NA_KW = 16
NA_W = NA_HEADS * NA_HD
DA_HEADS = 4
DA_QK = 64
DA_VD = 2 * DA_QK
SSM_CH = 512
SSM_P = 16
SSM_G = SSM_CH // SSM_P
SSM_N = 64
SSM_S = SSM_G * SSM_N
CONV_CH = 512
CONV_K = 31
IN_W = 4608
N_BRANCH = 4
N_EXPERTS = 64
TOP_K = 8
EXPERT_F = 256
ROUTE_SCALE = 2.5
ROPE_BASE = 10000.0
EPS = 1e-6
NEG_INF = -1e30

T_CTX = BATCH * SEQ
T_LAT = DEC_BATCH * DEC_SEQ
T_ALL = T_CTX + T_LAT

COL_NAQ, COL_NAK, COL_NAV, COL_DAQ, COL_DAK, COL_DAV, COL_U, COL_CA, COL_CG = range(9)

LANES = 128
ROWS_NA_Q = 4
ROWS_NA_K = 12
S5_ROWS = 512
VMEM_BIG = 56 * 1024 * 1024


def _params(n_axes, vmem=None):
    return pltpu.CompilerParams(dimension_semantics=("arbitrary",) * n_axes, vmem_limit_bytes=vmem)


def _silu(x):
    return x * jax.nn.sigmoid(x)


def _ada_kernel(c_ref, w_ref, b_ref, o_ref):
    c = c_ref[...]
    o_ref[0] = jnp.dot(_silu(c), w_ref[0], preferred_element_type=F32,
                       precision=lax.Precision.HIGHEST) + b_ref[0]


def _adaln(cvec, ada_w, ada_b):
    tn = 1536
    n = 6 * D_MODEL
    return pl.pallas_call(
        _ada_kernel,
        out_shape=jax.ShapeDtypeStruct((DEPTH, 8, n), F32),
        grid=(DEPTH, n // tn),
        in_specs=[pl.BlockSpec((8, D_MODEL), lambda l, j: (0, 0)),
                  pl.BlockSpec((1, D_MODEL, tn), lambda l, j: (l, 0, j)),
                  pl.BlockSpec((1, 1, tn), lambda l, j: (l, 0, j))],
        out_specs=pl.BlockSpec((1, 8, tn), lambda l, j: (l, 0, j)),
        compiler_params=_params(2, VMEM_BIG),
        name="adaln",
    )(cvec, ada_w, ada_b.reshape(DEPTH, 1, n))


def _mod_table(mods, tm):
    starts = np.arange(0, T_ALL, tm)
    idx = np.where(starts < T_CTX, 0, 1 + (starts - T_CTX) // DEC_SEQ)
    tab = mods[idx]
    return jnp.pad(tab, ((0, 0), (0, 2), (0, 0)))


def _in_kernel(x_ref, mod_ref, g_ref, w_ref, h_ref, z_ref):
    @pl.when(pl.program_id(1) == 0)
    def _():
        x = x_ref[...]
        y = x * lax.rsqrt(jnp.mean(x * x, axis=-1, keepdims=True) + EPS) * g_ref[...]
        h = y * (1.0 + mod_ref[0, 1:2, :]) + mod_ref[0, 0:1, :]
        h_ref[...] = h.astype(BF)

    z_ref[...] = jnp.dot(h_ref[...], w_ref[...], preferred_element_type=F32)


def _in_proj(x, modt, g, w):
    tm, tn = 512, 1536
    t = x.shape[0]
    return pl.pallas_call(
        _in_kernel,
        out_shape=(jax.ShapeDtypeStruct((t, D_MODEL), BF), jax.ShapeDtypeStruct((t, IN_W), F32)),
        grid=(t // tm, IN_W // tn),
        in_specs=[pl.BlockSpec((tm, D_MODEL), lambda i, j: (i, 0)),
                  pl.BlockSpec((1, 8, D_MODEL), lambda i, j: (i, 0, 0)),
                  pl.BlockSpec((1, D_MODEL), lambda i, j: (0, 0)),
                  pl.BlockSpec((D_MODEL, tn), lambda i, j: (0, j))],
        out_specs=(pl.BlockSpec((tm, D_MODEL), lambda i, j: (i, 0)),
                   pl.BlockSpec((tm, tn), lambda i, j: (i, j))),
        compiler_params=_params(2, VMEM_BIG),
        name="in_proj",
    )(x, modt, g.reshape(1, D_MODEL), w)


def _half_masks():
    lane = lax.broadcasted_iota(jnp.int32, (1, LANES), 1)
    lo = lane < (LANES // 2)
    return lo, jnp.logical_not(lo)


def _nt_dot(a, b):
    return lax.dot_general(a, b, (((1,), (1,)), ((), ())), preferred_element_type=F32)


def _softmax_parts(parts):
    m = jnp.max(parts[0], axis=-1, keepdims=True)
    for s in parts[1:]:
        m = jnp.maximum(m, jnp.max(s, axis=-1, keepdims=True))
    ps = [jnp.exp(s - m) for s in parts]
    den = jnp.sum(ps[0], axis=-1, keepdims=True)
    for p in ps[1:]:
        den = den + jnp.sum(p, axis=-1, keepdims=True)
    return ps, 1.0 / den


def _na_heads(q, k, v, o_ref, kc=None, vc=None, bias_ref=None):
    masks = _half_masks()
    for pair in range(NA_HEADS // 2):
        sl = slice(LANES * pair, LANES * (pair + 1))
        q128, k128, v128 = q[:, sl], k[:, sl], v[:, sl]
        o_pair = None
        for half in range(2):
            qm = jnp.where(masks[half], q128, 0.0).astype(BF)
            s = _nt_dot(qm, k128)
            if bias_ref is not None:
                s = s + bias_ref[0, 2 * pair + half]
            parts = [s]
            if kc is not None:
                parts.append(_nt_dot(qm, kc[:, sl]))
            ps, inv = _softmax_parts(parts)
            o = jnp.dot(ps[0].astype(BF), v128, preferred_element_type=F32)
            if kc is not None:
                o = o + jnp.dot(ps[1].astype(BF), vc[:, sl], preferred_element_type=F32)
            o = o * inv
            o_pair = o if half == 0 else jnp.where(masks[0], o_pair, o)
        o_ref[:, sl] = o_pair.astype(o_ref.dtype)


def _da_heads(q, k, v, lam, subg, out_scale, o_ref, kc=None, vc=None):
    masks = _half_masks()
    for h in range(DA_HEADS):
        sl = slice(LANES * h, LANES * (h + 1))
        q128, k128, v128 = q[:, sl], k[:, sl], v[:, sl]
        probs = []
        for m in range(2):
            qm = jnp.where(masks[m], q128, 0.0).astype(BF)
            parts = [_nt_dot(qm, k128)]
            if kc is not None:
                parts.append(_nt_dot(qm, kc[:, sl]))
            probs.append(_softmax_parts(parts))
        (p0, inv0), (p1, inv1) = probs
        r1 = lam * inv1
        a = (p0[0] * inv0 - p1[0] * r1).astype(BF)
        o = jnp.dot(a, v128, preferred_element_type=F32)
        if kc is not None:
            ac = (p0[1] * inv0 - p1[1] * r1).astype(BF)
            o = o + jnp.dot(ac, vc[:, sl], preferred_element_type=F32)
        o = o * lax.rsqrt(jnp.mean(o * o, axis=-1, keepdims=True) + EPS) * subg * out_scale
        o_ref[:, sl] = o.astype(o_ref.dtype)


def _attn_ctx_kernel(lam_ref, naq_ref, nak_ref, nav_ref, daq_ref, dak_ref, dav_ref, subg_ref,
                     ona_ref, oda_ref, *, out_scale):
    _na_heads(naq_ref[...] * (NA_HD ** -0.5), nak_ref[...].astype(BF), nav_ref[...].astype(BF), ona_ref)
    _da_heads(daq_ref[...] * (DA_QK ** -0.5), dak_ref[...].astype(BF), dav_ref[...].astype(BF),
              lam_ref[0, 0], subg_ref[...], out_scale, oda_ref)


def _attn_ctx(z, lam, subg, out_scale):
    def col(c):
        return pl.BlockSpec((SEQ, 512), lambda b: (b, c))

    return pl.pallas_call(
        functools.partial(_attn_ctx_kernel, out_scale=out_scale),
        out_shape=(jax.ShapeDtypeStruct((T_CTX, 512), BF), jax.ShapeDtypeStruct((T_CTX, 512), BF)),
        grid=(BATCH,),
        in_specs=[pl.BlockSpec(memory_space=pltpu.SMEM),
                  col(COL_NAQ), col(COL_NAK), col(COL_NAV), col(COL_DAQ), col(COL_DAK), col(COL_DAV),
                  pl.BlockSpec((1, LANES), lambda b: (0, 0))],
        out_specs=(pl.BlockSpec((SEQ, 512), lambda b: (b, 0)), pl.BlockSpec((SEQ, 512), lambda b: (b, 0))),
        compiler_params=_params(1, VMEM_BIG),
        name="attn_ctx",
    )(lam, z, z, z, z, z, z, subg)


def _rope_tables():
    half = DA_QK // 2
    inv = ROPE_BASE ** (-jnp.arange(0, half, 2, dtype=F32) / half)
    t = jnp.arange(DEC_SEQ)
    pos = jnp.stack([t // GRID_W, t % GRID_W], axis=-1).astype(F32)
    ang = pos[:, :, None] * inv
    ang64 = jnp.broadcast_to(ang[:, :, None, :], (DEC_SEQ, 2, 2, half // 2)).reshape(DEC_SEQ, DA_QK)
    sign = jnp.tile(jnp.repeat(jnp.array([-1.0, 1.0], F32), half // 2), 2)
    cos = jnp.tile(jnp.cos(ang64), (1, 512 // DA_QK))
    sin = jnp.tile(jnp.sin(ang64) * sign, (1, 512 // DA_QK))
    return cos, sin


def _rope_kernel(q_ref, k_ref, cos_ref, sin_ref, qo_ref, ko_ref):
    lane = lax.broadcasted_iota(jnp.int32, (1, 512), 1)
    first = (lane % (DA_QK // 2)) < (DA_QK // 4)
    cos, sin = cos_ref[...], sin_ref[...]

    def rope(x):
        partner = jnp.where(first, pltpu.roll(x, 512 - DA_QK // 4, 1), pltpu.roll(x, DA_QK // 4, 1))
        return x * cos + partner * sin

    qo_ref[...] = (rope(q_ref[...]) * (DA_QK ** -0.5)).astype(BF)
    ko_ref[...] = rope(k_ref[...]).astype(BF)


def _rope_lat(z, cos, sin):
    tm = 512
    base = T_CTX // tm
    per_seq = DEC_SEQ // tm
    return pl.pallas_call(
        _rope_kernel,
        out_shape=(jax.ShapeDtypeStruct((T_LAT, 512), BF), jax.ShapeDtypeStruct((T_LAT, 512), BF)),
        grid=(T_LAT // tm,),
        in_specs=[pl.BlockSpec((tm, 512), lambda i: (base + i, COL_DAQ)),
                  pl.BlockSpec((tm, 512), lambda i: (base + i, COL_DAK)),
                  pl.BlockSpec((tm, 512), lambda i: (i % per_seq, 0)),
                  pl.BlockSpec((tm, 512), lambda i: (i % per_seq, 0))],
        out_specs=(pl.BlockSpec((tm, 512), lambda i: (i, 0)), pl.BlockSpec((tm, 512), lambda i: (i, 0))),
        compiler_params=_params(1),
        name="rope_lat",
    )(z, z, cos, sin)


def _da_lat_kernel(lam_ref, q_ref, k_ref, v_ref, kc_ref, vc_ref, subg_ref, o_ref, *, out_scale):
    _da_heads(q_ref[...].astype(F32), k_ref[...], v_ref[...].astype(BF), lam_ref[0, 0], subg_ref[...],
              out_scale, o_ref, kc=kc_ref[0].astype(BF), vc=vc_ref[0].astype(BF))


def _da_lat(qr, kr, z, kc, vc, lam, subg, out_scale):
    tq = 256
    nq = DEC_SEQ // tq
    zb = T_CTX // DEC_SEQ
    return pl.pallas_call(
        functools.partial(_da_lat_kernel, out_scale=out_scale),
        out_shape=jax.ShapeDtypeStruct((T_LAT, 512), BF),
        grid=(DEC_BATCH, nq),
        in_specs=[pl.BlockSpec(memory_space=pltpu.SMEM),
                  pl.BlockSpec((tq, 512), lambda b, i: (b * nq + i, 0)),
                  pl.BlockSpec((DEC_SEQ, 512), lambda b, i: (b, 0)),
                  pl.BlockSpec((DEC_SEQ, 512), lambda b, i: (zb + b, COL_DAV)),
                  pl.BlockSpec((1, PAST_LEN, 512), lambda b, i: (b, 0, 0)),
                  pl.BlockSpec((1, PAST_LEN, 512), lambda b, i: (b, 0, 0)),
                  pl.BlockSpec((1, LANES), lambda b, i: (0, 0))],
        out_specs=pl.BlockSpec((tq, 512), lambda b, i: (b * nq + i, 0)),
        compiler_params=_params(2, VMEM_BIG),
        name="da_lat",
    )(lam, qr, kr, z, kc, vc, subg)


def _na_bias_table(rpb):
    rows = DEC_SEQ // GRID_W
    qc = np.arange(GRID_W)[:, None]
    kc = np.arange(GRID_W)[None, :]
    win0 = np.clip(qc - NA_KW // 2, 0, GRID_W - NA_KW)
    col_ok = (kc >= win0) & (kc < win0 + NA_KW)
    d_col = np.clip(kc - qc, 1 - NA_KW, NA_KW - 1) + (NA_KW - 1)
    pick = (d_col[:, :, None] == np.arange(2 * NA_KW - 1)).astype(np.float32)
    slabs = jnp.einsum('hij,qkj->hiqk', rpb.astype(F32), pick, precision=lax.Precision.HIGHEST)
    slabs = jnp.where(col_ok[None, None], slabs, NEG_INF)
    masked = jnp.full((NA_HEADS, GRID_W, GRID_W), NEG_INF, F32)
    variants = []
    for r_first, r0 in ((0, 0), (ROWS_NA_Q, 0), (rows - ROWS_NA_Q, rows - ROWS_NA_K)):
        per_q = []
        for rho in range(ROWS_NA_Q):
            qr = r_first + rho
            row0 = min(max(qr - NA_KH // 2, 0), rows - NA_KH)
            per_k = []
            for kap in range(ROWS_NA_K):
                kr = r0 + kap
                per_k.append(slabs[:, kr - qr + NA_KH - 1] if row0 <= kr < row0 + NA_KH else masked)
            per_q.append(jnp.stack(per_k, axis=2))
        variants.append(jnp.stack(per_q, axis=1))
    table = jnp.stack(variants, axis=0)
    return table.reshape(3, NA_HEADS, ROWS_NA_Q * GRID_W, ROWS_NA_K * GRID_W)


def _na_lat_kernel(q_ref, k_ref, v_ref, kc_ref, vc_ref, bias_ref, o_ref):
    g = pl.program_id(1)
    rows = DEC_SEQ // GRID_W
    r0 = jnp.clip(ROWS_NA_Q * g - NA_KH // 2, 0, rows - ROWS_NA_K)
    start = pl.multiple_of(r0 * GRID_W, GRID_W)
    nk = ROWS_NA_K * GRID_W
    k = k_ref[pl.ds(start, nk), :].astype(BF)
    v = v_ref[pl.ds(start, nk), :].astype(BF)
    _na_heads(q_ref[...] * (NA_HD ** -0.5), k, v, o_ref, kc=kc_ref[0].astype(BF), vc=vc_ref[0].astype(BF),
              bias_ref=bias_ref)


def _na_lat(z, kc, vc, bias):
    tq = ROWS_NA_Q * GRID_W
    ng = DEC_SEQ // tq
    nk = ROWS_NA_K * GRID_W
    qb = T_CTX // tq
    zb = T_CTX // DEC_SEQ

    def variant(g):
        return jnp.where(g == 0, 0, jnp.where(g == ng - 1, 2, 1))

    return pl.pallas_call(
        _na_lat_kernel,
        out_shape=jax.ShapeDtypeStruct((T_LAT, 512), BF),
        grid=(DEC_BATCH, ng),
        in_specs=[pl.BlockSpec((tq, 512), lambda b, g: (qb + b * ng + g, COL_NAQ)),
                  pl.BlockSpec((DEC_SEQ, 512), lambda b, g: (zb + b, COL_NAK)),
                  pl.BlockSpec((DEC_SEQ, 512), lambda b, g: (zb + b, COL_NAV)),
                  pl.BlockSpec((1, PAST_LEN, 512), lambda b, g: (b, 0, 0)),
                  pl.BlockSpec((1, PAST_LEN, 512), lambda b, g: (b, 0, 0)),
                  pl.BlockSpec((1, NA_HEADS, tq, nk), lambda b, g: (variant(g), 0, 0, 0))],
        out_specs=pl.BlockSpec((tq, 512), lambda b, g: (b * ng + g, 0)),
        compiler_params=_params(2, VMEM_BIG),
        name="na_lat",
    )(z, z, z, kc, vc, bias)


def _s5_discretize(lam_re, lam_im, log_dt, b_re, b_im):
    dt = jnp.exp(log_dt.astype(F32))[:, None]
    lr = lam_re.astype(F32)
    li = lam_im.astype(F32)
    mag = jnp.exp(lr * dt)
    ar = mag * jnp.cos(li * dt)
    ai = mag * jnp.sin(li * dt)
    den = lr * lr + li * li
    fr = ((ar - 1.0) * lr + ai * li) / den
    fi = (ai * lr - (ar - 1.0) * li) / den
    br = b_re.astype(F32)
    bi = b_im.astype(F32)
    bbr = fr[..., None] * br - fi[..., None] * bi
    bbi = fr[..., None] * bi + fi[..., None] * br
    return ar, ai, bbr, bbi


def _s5_weights(ar, ai, bbr, bbi, c_re, c_im):
    eye = jnp.eye(8, dtype=F32)
    nblk = SSM_G // 8

    def in_w(bb):
        return jnp.einsum('jgnp,gh->jgphn', bb.reshape(nblk, 8, SSM_N, SSM_P), eye).reshape(nblk, LANES, 8 * SSM_N)

    def out_w(c):
        return jnp.einsum('jgpn,gh->jgnhp', c.astype(F32).reshape(nblk, 8, SSM_P, SSM_N), eye).reshape(
            nblk, 8 * SSM_N, LANES)

    wb = jnp.concatenate([in_w(bbr), in_w(bbi)], axis=-1).astype(BF)
    wc = jnp.concatenate([out_w(c_re), -out_w(c_im)], axis=1).astype(BF)
    a_row = jnp.concatenate([ar.reshape(1, SSM_S), ai.reshape(1, SSM_S)], axis=1)
    return wb, wc, a_row


def _s5_kernel(uf_ref, ub_ref, wbf_ref, wbb_ref, wcf_ref, wcb_ref, af_ref, ab_ref, h0f_ref, h0b_ref,
               yf_ref, yb_ref, hff_ref, hfb_ref, hf_scr, hb_scr, *, nc, tc):
    r = nc * tc
    nblk = SSM_G // 8
    sb = 8 * SSM_N

    @pl.when(pl.program_id(0) == 0)
    def _():
        hf_scr[0:nc, :] = h0f_ref[...]
        hb_scr[r:r + nc, :] = h0b_ref[...]

    for j in range(nblk):
        cs = slice(LANES * j, LANES * (j + 1))
        pf = jnp.dot(uf_ref[:, cs], wbf_ref[j], preferred_element_type=F32)
        hf_scr[nc:nc + r, sb * j:sb * (j + 1)] = pf[:, :sb]
        hf_scr[nc:nc + r, SSM_S + sb * j:SSM_S + sb * (j + 1)] = pf[:, sb:]
        pb = jnp.dot(ub_ref[:, cs], wbb_ref[j], preferred_element_type=F32)
        hb_scr[0:r, sb * j:sb * (j + 1)] = pb[:, :sb]
        hb_scr[0:r, SSM_S + sb * j:SSM_S + sb * (j + 1)] = pb[:, sb:]

    def step(t, carry):
        f_prev = pl.multiple_of(t * nc, nc)
        f_cur = pl.multiple_of((t + 1) * nc, nc)
        b_cur = pl.multiple_of((tc - 1 - t) * nc, nc)
        b_prev = pl.multiple_of((tc - t) * nc, nc)
        for scr, a_ref, prev, cur in ((hf_scr, af_ref, f_prev, f_cur), (hb_scr, ab_ref, b_prev, b_cur)):
            for lb in range(nblk):
                lr = slice(sb * lb, sb * (lb + 1))
                li = slice(SSM_S + sb * lb, SSM_S + sb * (lb + 1))
                hr, hi = scr[pl.ds(prev, nc), lr], scr[pl.ds(prev, nc), li]
                ar, ai = a_ref[:, lr], a_ref[:, li]
                scr[pl.ds(cur, nc), lr] = ar * hr - ai * hi + scr[pl.ds(cur, nc), lr]
                scr[pl.ds(cur, nc), li] = ar * hi + ai * hr + scr[pl.ds(cur, nc), li]
        return carry

    lax.fori_loop(0, tc, step, 0)

    for j in range(nblk):
        cs = slice(LANES * j, LANES * (j + 1))
        lr = slice(sb * j, sb * (j + 1))
        li = slice(SSM_S + sb * j, SSM_S + sb * (j + 1))
        yf_ref[:, cs] = (jnp.dot(hf_scr[nc:nc + r, lr].astype(BF), wcf_ref[j, :sb], preferred_element_type=F32)
                         + jnp.dot(hf_scr[nc:nc + r, li].astype(BF), wcf_ref[j, sb:], preferred_element_type=F32))
        yb_ref[:, cs] = (jnp.dot(hb_scr[0:r, lr].astype(BF), wcb_ref[j, :sb], preferred_element_type=F32)
                         + jnp.dot(hb_scr[0:r, li].astype(BF), wcb_ref[j, sb:], preferred_element_type=F32))

    last_f = hf_scr[r:r + nc, :]
    last_b = hb_scr[0:nc, :]
    hf_scr[0:nc, :] = last_f
    hb_scr[r:r + nc, :] = last_b
    hff_ref[...] = last_f
    hfb_ref[...] = last_b


def _s5_scan(u_tm, nc, wts_f, wts_b, h0f, h0b):
    rows = u_tm.shape[0]
    tc = S5_ROWS // nc
    n_chunks = rows // S5_ROWS
    wbf, wcf, af = wts_f
    wbb, wcb, ab = wts_b
    af = jnp.broadcast_to(af, (nc, 2 * SSM_S))
    ab = jnp.broadcast_to(ab, (nc, 2 * SSM_S))
    const3 = lambda c: (0, 0, 0)
    const2 = lambda c: (0, 0)
    return pl.pallas_call(
        functools.partial(_s5_kernel, nc=nc, tc=tc),
        out_shape=(jax.ShapeDtypeStruct((rows, SSM_CH), F32), jax.ShapeDtypeStruct((rows, SSM_CH), F32),
                   jax.ShapeDtypeStruct((nc, 2 * SSM_S), F32), jax.ShapeDtypeStruct((nc, 2 * SSM_S), F32)),
        grid=(n_chunks,),
        in_specs=[pl.BlockSpec((S5_ROWS, SSM_CH), lambda c: (c, 0)),
                  pl.BlockSpec((S5_ROWS, SSM_CH), lambda c: (n_chunks - 1 - c, 0)),
                  pl.BlockSpec(wbf.shape, const3), pl.BlockSpec(wbb.shape, const3),
                  pl.BlockSpec(wcf.shape, const3), pl.BlockSpec(wcb.shape, const3),
                  pl.BlockSpec((nc, 2 * SSM_S), const2), pl.BlockSpec((nc, 2 * SSM_S), const2),
                  pl.BlockSpec((nc, 2 * SSM_S), const2), pl.BlockSpec((nc, 2 * SSM_S), const2)],
        out_specs=(pl.BlockSpec((S5_ROWS, SSM_CH), lambda c: (c, 0)),
                   pl.BlockSpec((S5_ROWS, SSM_CH), lambda c: (n_chunks - 1 - c, 0)),
                   pl.BlockSpec((nc, 2 * SSM_S), const2), pl.BlockSpec((nc, 2 * SSM_S), const2)),
        scratch_shapes=[pltpu.VMEM((S5_ROWS + nc, 2 * SSM_S), F32), pltpu.VMEM((S5_ROWS + nc, 2 * SSM_S), F32)],
        compiler_params=_params(1, VMEM_BIG),
        name="s5_scan",
    )(u_tm, u_tm, wbf, wbb, wcf, wcb, af, ab, h0f, h0b)


def _ssm_glu_kernel(u_ref, y_ref, d_ref, w_ref, o_ref):
    y = jax.nn.gelu(u_ref[...] * d_ref[...] + y_ref[...])
    o_ref[...] = (y * jax.nn.sigmoid(jnp.dot(y.astype(BF), w_ref[...], preferred_element_type=F32))).astype(BF)


def _ssm_glu(z, y_scan, d, w_glu):
    tm = 1024
    return pl.pallas_call(
        _ssm_glu_kernel,
        out_shape=jax.ShapeDtypeStruct((T_ALL, SSM_CH), BF),
        grid=(T_ALL // tm,),
        in_specs=[pl.BlockSpec((tm, SSM_CH), lambda i: (i, COL_U)),
                  pl.BlockSpec((tm, SSM_CH), lambda i: (i, 0)),
                  pl.BlockSpec((1, SSM_CH), lambda i: (0, 0)),
                  pl.BlockSpec((SSM_CH, SSM_CH), lambda i: (0, 0))],
        out_specs=pl.BlockSpec((tm, SSM_CH), lambda i: (i, 0)),
        compiler_params=_params(1),
        name="ssm_glu",
    )(z, y_scan, d.reshape(1, SSM_CH), w_glu)


CONV_PAD = 16
CONV_ROWS = 64


def _conv_kernel(a_ref, g_ref, w_ref, b_ref, lg_ref, lb_ref, o_ref, z_scr, *, seq):
    z_scr[0:CONV_PAD, :] = jnp.zeros((CONV_PAD, CONV_CH), F32)
    z_scr[CONV_PAD + seq:2 * CONV_PAD + seq, :] = jnp.zeros((CONV_PAD, CONV_CH), F32)
    z_scr[CONV_PAD:CONV_PAD + seq, :] = a_ref[...] * jax.nn.sigmoid(g_ref[...])
    off = CONV_PAD - CONV_K // 2

    def tile(i, carry):
        base = pl.multiple_of(i * CONV_ROWS, CONV_ROWS)
        win = z_scr[pl.ds(base, CONV_ROWS + 2 * CONV_PAD), :]
        acc = jnp.zeros((CONV_ROWS, CONV_CH), F32)
        for k in range(CONV_K):
            acc = acc + win[off + k:off + k + CONV_ROWS, :] * w_ref[k:k + 1, :]
        acc = acc + b_ref[...]
        mu = jnp.mean(acc, axis=-1, keepdims=True)
        cen = acc - mu
        var = jnp.mean(cen * cen, axis=-1, keepdims=True)
        y = cen * lax.rsqrt(var + EPS) * lg_ref[...] + lb_ref[...]
        o_ref[pl.ds(base, CONV_ROWS), :] = _silu(y).astype(BF)
        return carry

    lax.fori_loop(0, seq // CONV_ROWS, tile, 0)


def _conv_module(z, seq, row_block0, n_seq, w, b, lg, lb):
    row = lambda v: v.reshape(1, CONV_CH)
    wpad = jnp.pad(w, ((0, 32 - CONV_K), (0, 0)))
    vec = pl.BlockSpec((1, CONV_CH), lambda s: (0, 0))
    return pl.pallas_call(
        functools.partial(_conv_kernel, seq=seq),
        out_shape=jax.ShapeDtypeStruct((n_seq * seq, CONV_CH), BF),
        grid=(n_seq,),
        in_specs=[pl.BlockSpec((seq, CONV_CH), lambda s: (row_block0 + s, COL_CA)),
                  pl.BlockSpec((seq, CONV_CH), lambda s: (row_block0 + s, COL_CG)),
                  pl.BlockSpec((32, CONV_CH), lambda s: (0, 0)), vec, vec, vec],
        out_specs=pl.BlockSpec((seq, CONV_CH), lambda s: (s, 0)),
        scratch_shapes=[pltpu.VMEM((seq + 2 * CONV_PAD, CONV_CH), F32)],
        compiler_params=_params(1, VMEM_BIG),
        name="conv_module",
    )(z, z, wpad, row(b), row(lg), row(lb))


def _merge_kernel(h_ref, o_ref, wg_ref, wb_ref, y_ref, acc_ref):
    j = pl.program_id(2)
    gate = jax.nn.sigmoid(jnp.dot(h_ref[...], wg_ref[...], preferred_element_type=F32))
    t = gate * jnp.dot(o_ref[0], wb_ref[0], preferred_element_type=F32)

    @pl.when(j == 0)
    def _():
        acc_ref[...] = t

    @pl.when(j > 0)
    def _():
        acc_ref[...] += t

    @pl.when(j == N_BRANCH - 1)
    def _():
        y_ref[...] = acc_ref[...].astype(BF)


def _merge(h, branches, w_gate, w_br):
    tm, tn = 512, 1024
    nct = D_MODEL // tn
    return pl.pallas_call(
        _merge_kernel,
        out_shape=jax.ShapeDtypeStruct((T_ALL, D_MODEL), BF),
        grid=(T_ALL // tm, nct, N_BRANCH),
        in_specs=[pl.BlockSpec((tm, D_MODEL), lambda i, c, j: (i, 0)),
                  pl.BlockSpec((1, tm, 512), lambda i, c, j: (j, i, 0)),
                  pl.BlockSpec((D_MODEL, tn), lambda i, c, j: (0, j * nct + c)),
                  pl.BlockSpec((1, 512, tn), lambda i, c, j: (j, 0, c))],
        out_specs=pl.BlockSpec((tm, tn), lambda i, c, j: (i, c)),
        scratch_shapes=[pltpu.VMEM((tm, tn), F32)],
        compiler_params=_params(3, VMEM_BIG),
        name="merge",
    )(h, branches, w_gate, w_br)


def _pack_halves(v):
    k = v.shape[1] // 2
    bits = lax.bitcast_convert_type(v.astype(BF).astype(F32), jnp.uint32)
    return (bits[:, :k] >> 16) | bits[:, k:]


def _unpack_halves(w):
    lo = lax.bitcast_convert_type(w << 16, F32)
    hi = lax.bitcast_convert_type(w & jnp.uint32(0xFFFF0000), F32)
    return lo, hi


def _out_router_kernel(y_ref, w_ref, x_ref, mod_ref, g_ref, rh_ref, rl_ref, rb_ref,
                       xo_ref, h_ref, hp_ref, idx_ref, wts_ref):
    x = x_ref[...] + mod_ref[0, 2:3, :] * jnp.dot(y_ref[...], w_ref[...], preferred_element_type=F32)
    xo_ref[...] = x
    n = x * lax.rsqrt(jnp.mean(x * x, axis=-1, keepdims=True) + EPS) * g_ref[...]
    h = n * (1.0 + mod_ref[0, 4:5, :]) + mod_ref[0, 3:4, :]
    hb = h.astype(BF)
    h_ref[...] = hb
    hp_ref[...] = _pack_halves(h)
    h_lo = (h - hb.astype(F32)).astype(BF)
    logits = (jnp.dot(hb, rh_ref[...], preferred_element_type=F32)
              + jnp.dot(h_lo, rh_ref[...], preferred_element_type=F32)
              + jnp.dot(hb, rl_ref[...], preferred_element_type=F32))
    scores = jax.nn.sigmoid(logits)
    lane = lax.broadcasted_iota(jnp.int32, scores.shape, 1)
    cand = jnp.where(lane < N_EXPERTS, scores + rb_ref[...], -jnp.inf)
    idx = jnp.zeros(scores.shape, jnp.int32)
    wts = jnp.zeros(scores.shape, F32)
    total = jnp.zeros((scores.shape[0], 1), F32)
    for k in range(TOP_K):
        m = jnp.max(cand, axis=-1, keepdims=True)
        first = jnp.min(jnp.where(cand == m, lane, LANES), axis=-1, keepdims=True)
        hit = lane == first
        s_k = jnp.sum(jnp.where(hit, scores, 0.0), axis=-1, keepdims=True)
        idx = jnp.where(lane == k, first, idx)
        wts = jnp.where(lane == k, s_k, wts)
        total = total + s_k
        cand = jnp.where(hit, -jnp.inf, cand)
    idx_ref[...] = idx
    wts_ref[...] = wts / total * ROUTE_SCALE


def _out_router(y, w_out, x, modt, g2, r_hi, r_lo, r_bias):
    tm = 512
    full = lambda i: (0, 0)
    rows = lambda width: pl.BlockSpec((tm, width), lambda i: (i, 0))
    return pl.pallas_call(
        _out_router_kernel,
        out_shape=(jax.ShapeDtypeStruct((T_ALL, D_MODEL), F32), jax.ShapeDtypeStruct((T_ALL, D_MODEL), BF),
                   jax.ShapeDtypeStruct((T_ALL, D_MODEL // 2), jnp.uint32),
                   jax.ShapeDtypeStruct((T_ALL, LANES), jnp.int32), jax.ShapeDtypeStruct((T_ALL, LANES), F32)),
        grid=(T_ALL // tm,),
        in_specs=[rows(D_MODEL),
                  pl.BlockSpec((D_MODEL, D_MODEL), full),
                  rows(D_MODEL),
                  pl.BlockSpec((1, 8, D_MODEL), lambda i: (i, 0, 0)),
                  pl.BlockSpec((1, D_MODEL), full),
                  pl.BlockSpec((D_MODEL, LANES), full), pl.BlockSpec((D_MODEL, LANES), full),
                  pl.BlockSpec((1, LANES), full)],
        out_specs=(rows(D_MODEL), rows(D_MODEL), rows(D_MODEL // 2), rows(LANES), rows(LANES)),
        compiler_params=_params(1, VMEM_BIG),
        name="out_router",
    )(y, w_out, x, modt, g2.reshape(1, D_MODEL), r_hi, r_lo, r_bias)


MOE_TILE = 256
MOE_PAIRS = T_ALL * TOP_K
MOE_SLOTS = -(-(MOE_PAIRS + N_EXPERTS * (MOE_TILE - 1)) // MOE_TILE) * MOE_TILE
MOE_NT = MOE_SLOTS // MOE_TILE
MOE_Y_ROWS = MOE_PAIRS + 2 * MOE_TILE


def _route_plan(idx):
    i32 = jnp.int32
    idx8 = idx[:, :TOP_K]
    onehot = (idx8[:, :, None] == jnp.arange(N_EXPERTS, dtype=i32)).astype(i32)
    member = jnp.sum(onehot, axis=1)
    csum = jnp.cumsum(member, axis=0)
    rank = csum - member
    padded = (csum[-1] + MOE_TILE - 1) // MOE_TILE * MOE_TILE
    ends = jnp.cumsum(padded)
    starts = ends - padded
    slot = jnp.sum((starts[None, None, :] + rank[:, None, :]) * onehot, axis=-1)
    slot = slot.T.reshape(-1)
    pairs = jnp.arange(MOE_PAIRS, dtype=i32)
    slots = jnp.arange(MOE_SLOTS, dtype=i32)
    src = jnp.zeros((MOE_SLOTS,), i32).at[slot].set(pairs % T_ALL, unique_indices=True)
    spare = MOE_PAIRS + (slots // MOE_TILE % 2) * MOE_TILE + slots % MOE_TILE
    dst = spare.at[slot].set(pairs, unique_indices=True)
    tile_start = jnp.arange(MOE_NT, dtype=i32) * MOE_TILE
    tile_e = jnp.clip(jnp.searchsorted(starts, tile_start, side='right') - 1, 0, N_EXPERTS - 1).astype(i32)
    tile_v = (tile_start < ends[-1]).astype(i32)
    return src.reshape(MOE_NT, 1, MOE_TILE), dst.reshape(MOE_NT, 1, MOE_TILE), tile_e, tile_v


def _gmm_kernel(te_ref, tv_ref, src_ref, srcn_ref, dst_ref, h_hbm, w1_ref, w3_ref, w2_ref, y_hbm,
                xbuf, ybuf, w13_scr, w2_scr, gsem, ssem):
    i = pl.program_id(0)
    n = pl.num_programs(0)
    slot = i % 2
    nxt = jnp.minimum(i + 1, n - 1)
    valid = tv_ref[i] == 1
    next_valid = jnp.logical_and(i + 1 < n, tv_ref[nxt] == 1)

    def gather(tok, r, s):
        return pltpu.make_async_copy(h_hbm.at[pl.ds(tok, 1), :], xbuf.at[s, pl.ds(r, 1), :], gsem.at[s])

    def scatter(row, r, s):
        return pltpu.make_async_copy(ybuf.at[s, pl.ds(r, 1), :], y_hbm.at[pl.ds(row, 1), :], ssem.at[s])

    @pl.when(i == 0)
    def _():
        ybuf[...] = jnp.zeros(ybuf.shape, ybuf.dtype)
        for s in range(2):
            spare = pltpu.make_async_copy(ybuf.at[s], y_hbm.at[pl.ds(MOE_PAIRS + s * MOE_TILE, MOE_TILE), :],
                                          ssem.at[s])
            spare.start()
            spare.wait()

    @pl.when(jnp.logical_and(i == 0, valid))
    def _():
        for r in range(MOE_TILE):
            gather(src_ref[0, 0, r], r, 0).start()

    @pl.when(next_valid)
    def _():
        for r in range(MOE_TILE):
            gather(srcn_ref[0, 0, r], r, 1 - slot).start()

    @pl.when(valid)
    def _():
        @pl.when(jnp.logical_or(i == 0, te_ref[i] != te_ref[jnp.maximum(i - 1, 0)]))
        def _():
            w13_scr[:, :EXPERT_F] = w1_ref[0, 0].astype(BF)
            w13_scr[:, EXPERT_F:] = w3_ref[0, 0].astype(BF)
            w2_scr[...] = w2_ref[0, 0].astype(BF)

        for r in range(MOE_TILE):
            gather(0, r, slot).wait()
        lo, hi = _unpack_halves(xbuf[slot])
        half = D_MODEL // 2
        ab = (jnp.dot(lo.astype(BF), w13_scr[:half, :], preferred_element_type=F32)
              + jnp.dot(hi.astype(BF), w13_scr[half:, :], preferred_element_type=F32))
        zed = (_silu(ab[:, :EXPERT_F]) * ab[:, EXPERT_F:]).astype(BF)
        ybuf[slot] = _pack_halves(jnp.dot(zed, w2_scr[...], preferred_element_type=F32))
        for r in range(MOE_TILE):
            scatter(dst_ref[0, 0, r], r, slot).start()

        @pl.when(i > 0)
        def _():
            for r in range(MOE_TILE):
                scatter(0, r, 1 - slot).wait()

        @pl.when(jnp.logical_not(next_valid))
        def _():
            for r in range(MOE_TILE):
                scatter(0, r, slot).wait()


def _grouped_experts(hp, plan, w1, w3, w2, layer):
    src, dst, tile_e, tile_v = plan
    half = D_MODEL // 2
    smem_tile = lambda f: pl.BlockSpec((1, 1, MOE_TILE), f, memory_space=pltpu.SMEM)
    wspec = lambda shape: pl.BlockSpec((1, 1) + shape, lambda i, te, tv: (layer, te[i], 0, 0))
    grid_spec = pltpu.PrefetchScalarGridSpec(
        num_scalar_prefetch=2,
        grid=(MOE_NT,),
        in_specs=[smem_tile(lambda i, te, tv: (i, 0, 0)),
                  smem_tile(lambda i, te, tv: (jnp.minimum(i + 1, MOE_NT - 1), 0, 0)),
                  smem_tile(lambda i, te, tv: (i, 0, 0)),
                  pl.BlockSpec(memory_space=pl.ANY),
                  wspec((D_MODEL, EXPERT_F)), wspec((D_MODEL, EXPERT_F)), wspec((EXPERT_F, D_MODEL))],
        out_specs=pl.BlockSpec(memory_space=pl.ANY),
        scratch_shapes=[pltpu.VMEM((2, MOE_TILE, half), jnp.uint32), pltpu.VMEM((2, MOE_TILE, half), jnp.uint32),
                        pltpu.VMEM((D_MODEL, 2 * EXPERT_F), BF), pltpu.VMEM((EXPERT_F, D_MODEL), BF),
                        pltpu.SemaphoreType.DMA((2,)), pltpu.SemaphoreType.DMA((2,))])
    return pl.pallas_call(
        _gmm_kernel,
        out_shape=jax.ShapeDtypeStruct((MOE_Y_ROWS, half), jnp.uint32),
        grid_spec=grid_spec,
        compiler_params=_params(1, VMEM_BIG),
        name="grouped_experts",
    )(tile_e, tile_v, src, src, dst, hp, w1, w3, w2)


def _combine_kernel(x_ref, h_ref, wts_ref, mod_ref, w1_ref, w3_ref, w2_ref, *rest):
    y_refs, o_ref = rest[:TOP_K], rest[TOP_K]
    h = h_ref[...]
    a = jnp.dot(h, w1_ref[0].astype(BF), preferred_element_type=F32)
    b = jnp.dot(h, w3_ref[0].astype(BF), preferred_element_type=F32)
    acc = jnp.dot((_silu(a) * b).astype(BF), w2_ref[0].astype(BF), preferred_element_type=F32)
    half = D_MODEL // 2
    acc_lo, acc_hi = acc[:, :half], acc[:, half:]
    wts = wts_ref[...]
    for k in range(TOP_K):
        lo, hi = _unpack_halves(y_refs[k][...])
        w_k = wts[:, k:k + 1]
        acc_lo = acc_lo + w_k * lo
        acc_hi = acc_hi + w_k * hi
    gate = mod_ref[0, 5:6, :]
    o_ref[:, :half] = x_ref[:, :half] + gate[:, :half] * acc_lo
    o_ref[:, half:] = x_ref[:, half:] + gate[:, half:] * acc_hi


def _combine(x, h, wts, modt, sw1, sw3, sw2, y, layer):
    tm = 256
    nt = T_ALL // tm
    half = D_MODEL // 2
    rows = lambda width: pl.BlockSpec((tm, width), lambda i: (i, 0))
    y_specs = [pl.BlockSpec((tm, half), functools.partial(lambda i, k: (k * nt + i, 0), k=k)) for k in range(TOP_K)]
    return pl.pallas_call(
        _combine_kernel,
        out_shape=jax.ShapeDtypeStruct((T_ALL, D_MODEL), F32),
        grid=(nt,),
        in_specs=[rows(D_MODEL), rows(D_MODEL), rows(LANES),
                  pl.BlockSpec((1, 8, D_MODEL), lambda i: (i, 0, 0)),
                  pl.BlockSpec((1, D_MODEL, EXPERT_F), lambda i: (layer, 0, 0)),
                  pl.BlockSpec((1, D_MODEL, EXPERT_F), lambda i: (layer, 0, 0)),
                  pl.BlockSpec((1, EXPERT_F, D_MODEL), lambda i: (layer, 0, 0))] + y_specs,
        out_specs=rows(D_MODEL),
        compiler_params=_params(1, VMEM_BIG),
        name="moe_combine",
    )(x, h, wts, modt, sw1, sw3, sw2, *([y] * TOP_K))


def _final_kernel(x_ref, g_ref, o_ref):
    x = x_ref[...]
    o_ref[...] = x * lax.rsqrt(jnp.mean(x * x, axis=-1, keepdims=True) + EPS) * g_ref[...]


def _final_norm(x, g):
    tm = 512
    return pl.pallas_call(
        _final_kernel,
        out_shape=jax.ShapeDtypeStruct((T_ALL, D_MODEL), F32),
        grid=(T_ALL // tm,),
        in_specs=[pl.BlockSpec((tm, D_MODEL), lambda i: (i, 0)), pl.BlockSpec((1, D_MODEL), lambda i: (0, 0))],
        out_specs=pl.BlockSpec((tm, D_MODEL), lambda i: (i, 0)),
        compiler_params=_params(1),
        name="final_norm",
    )(x, g.reshape(1, D_MODEL))


def _time_major(u, n_chain):
    b, l, c = u.shape
    u = jnp.transpose(u, (1, 0, 2))
    u = jnp.pad(u, ((0, 0), (0, n_chain - b), (0, 0)))
    return u.reshape(l * n_chain, c)


def _from_time_major(y, b, l, n_chain):
    return jnp.transpose(y.reshape(l, n_chain, -1)[:, :b], (1, 0, 2)).reshape(b * l, -1)


def _diff_lambda(lam_q, lam_k, lam_init):
    lq = lam_q.astype(F32)
    lk = lam_k.astype(F32)
    return jnp.exp(jnp.sum(lq[0] * lk[0])) - jnp.exp(jnp.sum(lq[1] * lk[1])) + lam_init


def kernel(x_prompt, x_sample, cache_na_k, cache_na_v, cache_da_k, cache_da_v, state_ssm, c, c_ctx, ada_w, ada_b, norm1_g, norm2_g, w_in, w_gate, na_rpb, da_lam_q, da_lam_k, da_subln_g, ssm_lam_re, ssm_lam_im, ssm_log_dt, ssm_b_re, ssm_b_im, ssm_c_re, ssm_c_im, ssm_d, ssm_w_glu, conv_w, conv_b, conv_ln_g, conv_ln_b, w_br_na, w_br_da, w_br_ssm, w_br_conv, w_out, moe_router, moe_bias, moe_w1, moe_w3, moe_w2, shared_w1, shared_w3, shared_w2, final_g):
    x = jnp.concatenate([x_prompt.reshape(T_CTX, D_MODEL), x_sample.reshape(T_LAT, D_MODEL)], axis=0)
    cvec = jnp.zeros((8, D_MODEL), F32).at[0].set(c_ctx).at[1:1 + DEC_BATCH].set(c)
    mods_all = _adaln(cvec, ada_w, ada_b)
    cos, sin = _rope_tables()
    lat_chains = 8

    na_ks, na_vs, da_ks, da_vs, ssm_states = [], [], [], [], []
    for layer in range(DEPTH):
        mods = mods_all[layer, :1 + DEC_BATCH].reshape(1 + DEC_BATCH, 6, D_MODEL)
        modt = _mod_table(mods, 512)
        h1, z = _in_proj(x, modt, norm1_g[layer], w_in[layer].astype(BF))

        zc = z[:T_CTX]
        na_ks.append(zc[:, 512 * COL_NAK:512 * (COL_NAK + 1)].reshape(BATCH, SEQ, NA_HEADS, NA_HD))
        na_vs.append(zc[:, 512 * COL_NAV:512 * (COL_NAV + 1)].reshape(BATCH, SEQ, NA_HEADS, NA_HD))
        da_ks.append(zc[:, 512 * COL_DAK:512 * (COL_DAK + 1)].reshape(BATCH, SEQ, DA_HEADS, 2, DA_QK))
        da_vs.append(zc[:, 512 * COL_DAV:512 * (COL_DAV + 1)].reshape(BATCH, SEQ, DA_HEADS, DA_VD))

        lam_init = 0.8 - 0.6 * math.exp(-0.3 * layer)
        lam = _diff_lambda(da_lam_q[layer], da_lam_k[layer], lam_init).reshape(1, 1)
        subg = da_subln_g[layer].reshape(1, DA_VD)
        ona_c, oda_c = _attn_ctx(z, lam, subg, 1.0 - lam_init)
        ona_l = _na_lat(z, cache_na_k[:, layer].reshape(DEC_BATCH, PAST_LEN, 512),
                        cache_na_v[:, layer].reshape(DEC_BATCH, PAST_LEN, 512), _na_bias_table(na_rpb[layer]))
        qr, kr = _rope_lat(z, cos, sin)
        oda_l = _da_lat(qr, kr, z, cache_da_k[:, layer].reshape(DEC_BATCH, PAST_LEN, 512),
                        cache_da_v[:, layer].reshape(DEC_BATCH, PAST_LEN, 512), lam, subg, 1.0 - lam_init)

        wts = []
        for d in range(2):
            ar, ai, bbr, bbi = _s5_discretize(ssm_lam_re[layer, d], ssm_lam_im[layer, d], ssm_log_dt[layer, d],
                                              ssm_b_re[layer, d], ssm_b_im[layer, d])
            wts.append(_s5_weights(ar, ai, bbr, bbi, ssm_c_re[layer, d], ssm_c_im[layer, d]))
        u = z[:, 512 * COL_U:512 * (COL_U + 1)].astype(BF)
        zeros_c = jnp.zeros((BATCH, 2 * SSM_S), F32)
        yf, yb, hf, hb = _s5_scan(_time_major(u[:T_CTX].reshape(BATCH, SEQ, SSM_CH), BATCH), BATCH,
                                  wts[0], wts[1], zeros_c, zeros_c)
        y_ctx = _from_time_major(yf + yb, BATCH, SEQ, BATCH)
        ssm_states.append(jnp.stack([hf.reshape(BATCH, 2, SSM_G, SSM_N), hb.reshape(BATCH, 2, SSM_G, SSM_N)], axis=1))
        h0 = state_ssm[:, layer].astype(F32).reshape(DEC_BATCH, 2, 2 * SSM_S)
        h0 = jnp.pad(h0, ((0, lat_chains - DEC_BATCH), (0, 0), (0, 0)))
        yf, yb, _, _ = _s5_scan(_time_major(u[T_CTX:].reshape(DEC_BATCH, DEC_SEQ, SSM_CH), lat_chains), lat_chains,
                                wts[0], wts[1], h0[:, 0], h0[:, 1])
        y_lat = _from_time_major(yf + yb, DEC_BATCH, DEC_SEQ, lat_chains)
        o_ssm = _ssm_glu(z, jnp.concatenate([y_ctx, y_lat], axis=0), ssm_d[layer], ssm_w_glu[layer].astype(BF))

        conv_args = (conv_w[layer], conv_b[layer], conv_ln_g[layer], conv_ln_b[layer])
        o_conv = jnp.concatenate([_conv_module(z, SEQ, 0, BATCH, *conv_args),
                                  _conv_module(z, DEC_SEQ, T_CTX // DEC_SEQ, DEC_BATCH, *conv_args)], axis=0)

        branches = jnp.stack([jnp.concatenate([ona_c, ona_l], axis=0), jnp.concatenate([oda_c, oda_l], axis=0),
                              o_ssm, o_conv], axis=0)
        w_br = jnp.stack([w_br_na[layer], w_br_da[layer], w_br_ssm[layer], w_br_conv[layer]], axis=0).astype(BF)
        y = _merge(h1, branches, w_gate[layer].astype(BF), w_br)

        router = jnp.pad(moe_router[layer], ((0, 0), (0, LANES - N_EXPERTS)))
        r_hi = router.astype(BF)
        r_lo = (router - r_hi.astype(F32)).astype(BF)
        r_bias = jnp.pad(moe_bias[layer].astype(F32), (0, LANES - N_EXPERTS)).reshape(1, LANES)
        x_mid, h2, h2p, idx, wts = _out_router(y, w_out[layer].astype(BF), x, modt, norm2_g[layer],
                                               r_hi, r_lo, r_bias)
        y_pairs = _grouped_experts(h2p, _route_plan(idx), moe_w1, moe_w3, moe_w2, layer)
        x = _combine(x_mid, h2, wts, _mod_table(mods, 256), shared_w1, shared_w3, shared_w2, y_pairs, layer)

    y = _final_norm(x, final_g)
    y_prompt = y[:T_CTX].reshape(BATCH, SEQ, D_MODEL)
    y_sample = y[T_CTX:].reshape(DEC_BATCH, DEC_SEQ, D_MODEL)
    return (y_prompt, y_sample, jnp.stack(na_ks, axis=1), jnp.stack(na_vs, axis=1), jnp.stack(da_ks, axis=1),
            jnp.stack(da_vs, axis=1), jnp.stack(ssm_states, axis=1))
```

```python
import functools
import math

import numpy as np
import jax
import jax.numpy as jnp
from jax import lax
from jax.experimental import pallas as pl
from jax.experimental.pallas import tpu as pltpu

F32 = jnp.float32
BF = jnp.bfloat16

D_MODEL = 2048
BATCH = 32
SEQ = 256
DEPTH = 2
DEC_BATCH = 2
DEC_SEQ = 2048
PAST_LEN = 256
GRID_W = 64
NA_HEADS = 8
NA_HD = 64
NA_KH = 8
NA_KW = 16
NA_W = NA_HEADS * NA_HD
DA_HEADS = 4
DA_QK = 64
DA_VD = 2 * DA_QK
SSM_CH = 512
SSM_P = 16
SSM_G = SSM_CH // SSM_P
SSM_N = 64
SSM_S = SSM_G * SSM_N
CONV_CH = 512
CONV_K = 31
IN_W = 4608
N_BRANCH = 4
N_EXPERTS = 64
TOP_K = 8
EXPERT_F = 256
ROUTE_SCALE = 2.5
ROPE_BASE = 10000.0
EPS = 1e-6
NEG_INF = -1e30

T_CTX = BATCH * SEQ
T_LAT = DEC_BATCH * DEC_SEQ
T_ALL = T_CTX + T_LAT

COL_NAQ, COL_NAK, COL_NAV, COL_DAQ, COL_DAK, COL_DAV, COL_U, COL_CA, COL_CG = range(9)

LANES = 128
ROWS_NA_Q = 4
ROWS_NA_K = 12
S5_ROWS = 512
VMEM_BIG = 56 * 1024 * 1024


def _params(n_axes, vmem=None):
    return pltpu.CompilerParams(dimension_semantics=("arbitrary",) * n_axes, vmem_limit_bytes=vmem)


def _silu(x):
    return x * jax.nn.sigmoid(x)


def _ada_kernel(c_ref, w_ref, b_ref, o_ref):
    c = c_ref[...]
    o_ref[0] = jnp.dot(_silu(c), w_ref[0], preferred_element_type=F32,
                       precision=lax.Precision.HIGHEST) + b_ref[0]


def _adaln(cvec, ada_w, ada_b):
    tn = 1536
    n = 6 * D_MODEL
    return pl.pallas_call(
        _ada_kernel,
        out_shape=jax.ShapeDtypeStruct((DEPTH, 8, n), F32),
        grid=(DEPTH, n // tn),
        in_specs=[pl.BlockSpec((8, D_MODEL), lambda l, j: (0, 0)),
                  pl.BlockSpec((1, D_MODEL, tn), lambda l, j: (l, 0, j)),
                  pl.BlockSpec((1, 1, tn), lambda l, j: (l, 0, j))],
        out_specs=pl.BlockSpec((1, 8, tn), lambda l, j: (l, 0, j)),
        compiler_params=_params(2, VMEM_BIG),
        name="adaln",
    )(cvec, ada_w, ada_b.reshape(DEPTH, 1, n))


def _mod_table(mods, tm):
    starts = np.arange(0, T_ALL, tm)
    idx = np.where(starts < T_CTX, 0, 1 + (starts - T_CTX) // DEC_SEQ)
    tab = mods[idx]
    return jnp.pad(tab, ((0, 0), (0, 2), (0, 0)))


def _in_kernel(x_ref, mod_ref, g_ref, w_ref, h_ref, z_ref):
    @pl.when(pl.program_id(1) == 0)
    def _():
        x = x_ref[...]
        y = x * lax.rsqrt(jnp.mean(x * x, axis=-1, keepdims=True) + EPS) * g_ref[...]
        h = y * (1.0 + mod_ref[0, 1:2, :]) + mod_ref[0, 0:1, :]
        h_ref[...] = h.astype(BF)

    z_ref[...] = jnp.dot(h_ref[...], w_ref[...], preferred_element_type=F32)


def _in_proj(x, modt, g, w):
    tm, tn = 512, 1536
    t = x.shape[0]
    return pl.pallas_call(
        _in_kernel,
        out_shape=(jax.ShapeDtypeStruct((t, D_MODEL), BF), jax.ShapeDtypeStruct((t, IN_W), F32)),
        grid=(t // tm, IN_W // tn),
        in_specs=[pl.BlockSpec((tm, D_MODEL), lambda i, j: (i, 0)),
                  pl.BlockSpec((1, 8, D_MODEL), lambda i, j: (i, 0, 0)),
                  pl.BlockSpec((1, D_MODEL), lambda i, j: (0, 0)),
                  pl.BlockSpec((D_MODEL, tn), lambda i, j: (0, j))],
        out_specs=(pl.BlockSpec((tm, D_MODEL), lambda i, j: (i, 0)),
                   pl.BlockSpec((tm, tn), lambda i, j: (i, j))),
        compiler_params=_params(2, VMEM_BIG),
        name="in_proj",
    )(x, modt, g.reshape(1, D_MODEL), w)


def _half_masks():
    lane = lax.broadcasted_iota(jnp.int32, (1, LANES), 1)
    lo = lane < (LANES // 2)
    return lo, jnp.logical_not(lo)


def _nt_dot(a, b):
    return lax.dot_general(a, b, (((1,), (1,)), ((), ())), preferred_element_type=F32)


def _softmax_parts(parts):
    m = jnp.max(parts[0], axis=-1, keepdims=True)
    for s in parts[1:]:
        m = jnp.maximum(m, jnp.max(s, axis=-1, keepdims=True))
    ps = [jnp.exp(s - m) for s in parts]
    den = jnp.sum(ps[0], axis=-1, keepdims=True)
    for p in ps[1:]:
        den = den + jnp.sum(p, axis=-1, keepdims=True)
    return ps, 1.0 / den


def _na_heads(q, k, v, o_ref, kc=None, vc=None, bias_ref=None):
    masks = _half_masks()
    for pair in range(NA_HEADS // 2):
        sl = slice(LANES * pair, LANES * (pair + 1))
        q128, k128, v128 = q[:, sl], k[:, sl], v[:, sl]
        o_pair = None
        for half in range(2):
            qm = jnp.where(masks[half], q128, 0.0).astype(BF)
            s = _nt_dot(qm, k128)
            if bias_ref is not None:
                s = s + bias_ref[0, 2 * pair + half]
            parts = [s]
            if kc is not None:
                parts.append(_nt_dot(qm, kc[:, sl]))
            ps, inv = _softmax_parts(parts)
            o = jnp.dot(ps[0].astype(BF), v128, preferred_element_type=F32)
            if kc is not None:
                o = o + jnp.dot(ps[1].astype(BF), vc[:, sl], preferred_element_type=F32)
            o = o * inv
            o_pair = o if half == 0 else jnp.where(masks[0], o_pair, o)
        o_ref[:, sl] = o_pair.astype(o_ref.dtype)


def _da_heads(q, k, v, lam, subg, out_scale, o_ref, kc=None, vc=None):
    masks = _half_masks()
    for h in range(DA_HEADS):
        sl = slice(LANES * h, LANES * (h + 1))
        q128, k128, v128 = q[:, sl], k[:, sl], v[:, sl]
        probs = []
        for m in range(2):
            qm = jnp.where(masks[m], q128, 0.0).astype(BF)
            parts = [_nt_dot(qm, k128)]
            if kc is not None:
                parts.append(_nt_dot(qm, kc[:, sl]))
            probs.append(_softmax_parts(parts))
        (p0, inv0), (p1, inv1) = probs
        r1 = lam * inv1
        a = (p0[0] * inv0 - p1[0] * r1).astype(BF)
        o = jnp.dot(a, v128, preferred_element_type=F32)
        if kc is not None:
            ac = (p0[1] * inv0 - p1[1] * r1).astype(BF)
            o = o + jnp.dot(ac, vc[:, sl], preferred_element_type=F32)
        o = o * lax.rsqrt(jnp.mean(o * o, axis=-1, keepdims=True) + EPS) * subg * out_scale
        o_ref[:, sl] = o.astype(o_ref.dtype)


def _attn_ctx_kernel(lam_ref, naq_ref, nak_ref, nav_ref, daq_ref, dak_ref, dav_ref, subg_ref,
                     ona_ref, oda_ref, *, out_scale):
    _na_heads(naq_ref[...] * (NA_HD ** -0.5), nak_ref[...].astype(BF), nav_ref[...].astype(BF), ona_ref)
    _da_heads(daq_ref[...] * (DA_QK ** -0.5), dak_ref[...].astype(BF), dav_ref[...].astype(BF),
              lam_ref[0, 0], subg_ref[...], out_scale, oda_ref)


def _attn_ctx(z, lam, subg, out_scale):
    def col(c):
        return pl.BlockSpec((SEQ, 512), lambda b: (b, c))

    return pl.pallas_call(
        functools.partial(_attn_ctx_kernel, out_scale=out_scale),
        out_shape=(jax.ShapeDtypeStruct((T_CTX, 512), BF), jax.ShapeDtypeStruct((T_CTX, 512), BF)),
        grid=(BATCH,),
        in_specs=[pl.BlockSpec(memory_space=pltpu.SMEM),
                  col(COL_NAQ), col(COL_NAK), col(COL_NAV), col(COL_DAQ), col(COL_DAK), col(COL_DAV),
                  pl.BlockSpec((1, LANES), lambda b: (0, 0))],
        out_specs=(pl.BlockSpec((SEQ, 512), lambda b: (b, 0)), pl.BlockSpec((SEQ, 512), lambda b: (b, 0))),
        compiler_params=_params(1, VMEM_BIG),
        name="attn_ctx",
    )(lam, z, z, z, z, z, z, subg)


def _rope_tables():
    half = DA_QK // 2
    inv = ROPE_BASE ** (-jnp.arange(0, half, 2, dtype=F32) / half)
    t = jnp.arange(DEC_SEQ)
    pos = jnp.stack([t // GRID_W, t % GRID_W], axis=-1).astype(F32)
    ang = pos[:, :, None] * inv
    ang64 = jnp.broadcast_to(ang[:, :, None, :], (DEC_SEQ, 2, 2, half // 2)).reshape(DEC_SEQ, DA_QK)
    sign = jnp.tile(jnp.repeat(jnp.array([-1.0, 1.0], F32), half // 2), 2)
    cos = jnp.tile(jnp.cos(ang64), (1, 512 // DA_QK))
    sin = jnp.tile(jnp.sin(ang64) * sign, (1, 512 // DA_QK))
    return cos, sin


def _rope_kernel(q_ref, k_ref, cos_ref, sin_ref, qo_ref, ko_ref):
    lane = lax.broadcasted_iota(jnp.int32, (1, 512), 1)
    first = (lane % (DA_QK // 2)) < (DA_QK // 4)
    cos, sin = cos_ref[...], sin_ref[...]

    def rope(x):
        partner = jnp.where(first, pltpu.roll(x, 512 - DA_QK // 4, 1), pltpu.roll(x, DA_QK // 4, 1))
        return x * cos + partner * sin

    qo_ref[...] = (rope(q_ref[...]) * (DA_QK ** -0.5)).astype(BF)
    ko_ref[...] = rope(k_ref[...]).astype(BF)


def _rope_lat(z, cos, sin):
    tm = 512
    base = T_CTX // tm
    per_seq = DEC_SEQ // tm
    return pl.pallas_call(
        _rope_kernel,
        out_shape=(jax.ShapeDtypeStruct((T_LAT, 512), BF), jax.ShapeDtypeStruct((T_LAT, 512), BF)),
        grid=(T_LAT // tm,),
        in_specs=[pl.BlockSpec((tm, 512), lambda i: (base + i, COL_DAQ)),
                  pl.BlockSpec((tm, 512), lambda i: (base + i, COL_DAK)),
                  pl.BlockSpec((tm, 512), lambda i: (i % per_seq, 0)),
                  pl.BlockSpec((tm, 512), lambda i: (i % per_seq, 0))],
        out_specs=(pl.BlockSpec((tm, 512), lambda i: (i, 0)), pl.BlockSpec((tm, 512), lambda i: (i, 0))),
        compiler_params=_params(1),
        name="rope_lat",
    )(z, z, cos, sin)


def _da_lat_kernel(lam_ref, q_ref, k_ref, v_ref, kc_ref, vc_ref, subg_ref, o_ref, *, out_scale):
    _da_heads(q_ref[...].astype(F32), k_ref[...], v_ref[...].astype(BF), lam_ref[0, 0], subg_ref[...],
              out_scale, o_ref, kc=kc_ref[0].astype(BF), vc=vc_ref[0].astype(BF))


def _da_lat(qr, kr, z, kc, vc, lam, subg, out_scale):
    tq = 256
    nq = DEC_SEQ // tq
    zb = T_CTX // DEC_SEQ
    return pl.pallas_call(
        functools.partial(_da_lat_kernel, out_scale=out_scale),
        out_shape=jax.ShapeDtypeStruct((T_LAT, 512), BF),
        grid=(DEC_BATCH, nq),
        in_specs=[pl.BlockSpec(memory_space=pltpu.SMEM),
                  pl.BlockSpec((tq, 512), lambda b, i: (b * nq + i, 0)),
                  pl.BlockSpec((DEC_SEQ, 512), lambda b, i: (b, 0)),
                  pl.BlockSpec((DEC_SEQ, 512), lambda b, i: (zb + b, COL_DAV)),
                  pl.BlockSpec((1, PAST_LEN, 512), lambda b, i: (b, 0, 0)),
                  pl.BlockSpec((1, PAST_LEN, 512), lambda b, i: (b, 0, 0)),
                  pl.BlockSpec((1, LANES), lambda b, i: (0, 0))],
        out_specs=pl.BlockSpec((tq, 512), lambda b, i: (b * nq + i, 0)),
        compiler_params=_params(2, VMEM_BIG),
        name="da_lat",
    )(lam, qr, kr, z, kc, vc, subg)


def _na_bias_table(rpb):
    rows = DEC_SEQ // GRID_W
    qc = np.arange(GRID_W)[:, None]
    kc = np.arange(GRID_W)[None, :]
    win0 = np.clip(qc - NA_KW // 2, 0, GRID_W - NA_KW)
    col_ok = (kc >= win0) & (kc < win0 + NA_KW)
    d_col = np.clip(kc - qc, 1 - NA_KW, NA_KW - 1) + (NA_KW - 1)
    pick = (d_col[:, :, None] == np.arange(2 * NA_KW - 1)).astype(np.float32)
    slabs = jnp.einsum('hij,qkj->hiqk', rpb.astype(F32), pick, precision=lax.Precision.HIGHEST)
    slabs = jnp.where(col_ok[None, None], slabs, NEG_INF)
    masked = jnp.full((NA_HEADS, GRID_W, GRID_W), NEG_INF, F32)
    variants = []
    for r_first, r0 in ((0, 0), (ROWS_NA_Q, 0), (rows - ROWS_NA_Q, rows - ROWS_NA_K)):
        per_q = []
        for rho in range(ROWS_NA_Q):
            qr = r_first + rho
            row0 = min(max(qr - NA_KH // 2, 0), rows - NA_KH)
            per_k = []
            for kap in range(ROWS_NA_K):
                kr = r0 + kap
                per_k.append(slabs[:, kr - qr + NA_KH - 1] if row0 <= kr < row0 + NA_KH else masked)
            per_q.append(jnp.stack(per_k, axis=2))
        variants.append(jnp.stack(per_q, axis=1))
    table = jnp.stack(variants, axis=0)
    return table.reshape(3, NA_HEADS, ROWS_NA_Q * GRID_W, ROWS_NA_K * GRID_W)


def _na_lat_kernel(q_ref, k_ref, v_ref, kc_ref, vc_ref, bias_ref, o_ref):
    g = pl.program_id(1)
    rows = DEC_SEQ // GRID_W
    r0 = jnp.clip(ROWS_NA_Q * g - NA_KH // 2, 0, rows - ROWS_NA_K)
    start = pl.multiple_of(r0 * GRID_W, GRID_W)
    nk = ROWS_NA_K * GRID_W
    k = k_ref[pl.ds(start, nk), :].astype(BF)
    v = v_ref[pl.ds(start, nk), :].astype(BF)
    _na_heads(q_ref[...] * (NA_HD ** -0.5), k, v, o_ref, kc=kc_ref[0].astype(BF), vc=vc_ref[0].astype(BF),
              bias_ref=bias_ref)


def _na_lat(z, kc, vc, bias):
    tq = ROWS_NA_Q * GRID_W
    ng = DEC_SEQ // tq
    nk = ROWS_NA_K * GRID_W
    qb = T_CTX // tq
    zb = T_CTX // DEC_SEQ

    def variant(g):
        return jnp.where(g == 0, 0, jnp.where(g == ng - 1, 2, 1))

    return pl.pallas_call(
        _na_lat_kernel,
        out_shape=jax.ShapeDtypeStruct((T_LAT, 512), BF),
        grid=(DEC_BATCH, ng),
        in_specs=[pl.BlockSpec((tq, 512), lambda b, g: (qb + b * ng + g, COL_NAQ)),
                  pl.BlockSpec((DEC_SEQ, 512), lambda b, g: (zb + b, COL_NAK)),
                  pl.BlockSpec((DEC_SEQ, 512), lambda b, g: (zb + b, COL_NAV)),
                  pl.BlockSpec((1, PAST_LEN, 512), lambda b, g: (b, 0, 0)),
                  pl.BlockSpec((1, PAST_LEN, 512), lambda b, g: (b, 0, 0)),
                  pl.BlockSpec((1, NA_HEADS, tq, nk), lambda b, g: (variant(g), 0, 0, 0))],
        out_specs=pl.BlockSpec((tq, 512), lambda b, g: (b * ng + g, 0)),
        compiler_params=_params(2, VMEM_BIG),
        name="na_lat",
    )(z, z, z, kc, vc, bias)


def _s5_discretize(lam_re, lam_im, log_dt, b_re, b_im):
    dt = jnp.exp(log_dt.astype(F32))[:, None]
    lr = lam_re.astype(F32)
    li = lam_im.astype(F32)
    mag = jnp.exp(lr * dt)
    ar = mag * jnp.cos(li * dt)
    ai = mag * jnp.sin(li * dt)
    den = lr * lr + li * li
    fr = ((ar - 1.0) * lr + ai * li) / den
    fi = (ai * lr - (ar - 1.0) * li) / den
    br = b_re.astype(F32)
    bi = b_im.astype(F32)
    bbr = fr[..., None] * br - fi[..., None] * bi
    bbi = fr[..., None] * bi + fi[..., None] * br
    return ar, ai, bbr, bbi


def _s5_weights(ar, ai, bbr, bbi, c_re, c_im):
    eye = jnp.eye(8, dtype=F32)
    nblk = SSM_G // 8

    def in_w(bb):
        return jnp.einsum('jgnp,gh->jgphn', bb.reshape(nblk, 8, SSM_N, SSM_P), eye).reshape(nblk, LANES, 8 * SSM_N)

    def out_w(c):
        return jnp.einsum('jgpn,gh->jgnhp', c.astype(F32).reshape(nblk, 8, SSM_P, SSM_N), eye).reshape(
            nblk, 8 * SSM_N, LANES)

    wb = jnp.concatenate([in_w(bbr), in_w(bbi)], axis=-1).astype(BF)
    wc = jnp.concatenate([out_w(c_re), -out_w(c_im)], axis=1).astype(BF)
    a_row = jnp.concatenate([ar.reshape(1, SSM_S), ai.reshape(1, SSM_S)], axis=1)
    return wb, wc, a_row


def _s5_kernel(uf_ref, ub_ref, wbf_ref, wbb_ref, wcf_ref, wcb_ref, af_ref, ab_ref, h0f_ref, h0b_ref,
               yf_ref, yb_ref, hff_ref, hfb_ref, hf_scr, hb_scr, *, nc, tc):
    r = nc * tc
    nblk = SSM_G // 8
    sb = 8 * SSM_N

    @pl.when(pl.program_id(0) == 0)
    def _():
        hf_scr[0:nc, :] = h0f_ref[...]
        hb_scr[r:r + nc, :] = h0b_ref[...]

    for j in range(nblk):
        cs = slice(LANES * j, LANES * (j + 1))
        pf = jnp.dot(uf_ref[:, cs], wbf_ref[j], preferred_element_type=F32)
        hf_scr[nc:nc + r, sb * j:sb * (j + 1)] = pf[:, :sb]
        hf_scr[nc:nc + r, SSM_S + sb * j:SSM_S + sb * (j + 1)] = pf[:, sb:]
        pb = jnp.dot(ub_ref[:, cs], wbb_ref[j], preferred_element_type=F32)
        hb_scr[0:r, sb * j:sb * (j + 1)] = pb[:, :sb]
        hb_scr[0:r, SSM_S + sb * j:SSM_S + sb * (j + 1)] = pb[:, sb:]

    def step(t, carry):
        f_prev = pl.multiple_of(t * nc, nc)
        f_cur = pl.multiple_of((t + 1) * nc, nc)
        b_cur = pl.multiple_of((tc - 1 - t) * nc, nc)
        b_prev = pl.multiple_of((tc - t) * nc, nc)
        for scr, a_ref, prev, cur in ((hf_scr, af_ref, f_prev, f_cur), (hb_scr, ab_ref, b_prev, b_cur)):
            for lb in range(nblk):
                lr = slice(sb * lb, sb * (lb + 1))
                li = slice(SSM_S + sb * lb, SSM_S + sb * (lb + 1))
                hr, hi = scr[pl.ds(prev, nc), lr], scr[pl.ds(prev, nc), li]
                ar, ai = a_ref[:, lr], a_ref[:, li]
                scr[pl.ds(cur, nc), lr] = ar * hr - ai * hi + scr[pl.ds(cur, nc), lr]
                scr[pl.ds(cur, nc), li] = ar * hi + ai * hr + scr[pl.ds(cur, nc), li]
        return carry

    lax.fori_loop(0, tc, step, 0)

    for j in range(nblk):
        cs = slice(LANES * j, LANES * (j + 1))
        lr = slice(sb * j, sb * (j + 1))
        li = slice(SSM_S + sb * j, SSM_S + sb * (j + 1))
        yf_ref[:, cs] = (jnp.dot(hf_scr[nc:nc + r, lr].astype(BF), wcf_ref[j, :sb], preferred_element_type=F32)
                         + jnp.dot(hf_scr[nc:nc + r, li].astype(BF), wcf_ref[j, sb:], preferred_element_type=F32))
        yb_ref[:, cs] = (jnp.dot(hb_scr[0:r, lr].astype(BF), wcb_ref[j, :sb], preferred_element_type=F32)
                         + jnp.dot(hb_scr[0:r, li].astype(BF), wcb_ref[j, sb:], preferred_element_type=F32))

    last_f = hf_scr[r:r + nc, :]
    last_b = hb_scr[0:nc, :]
    hf_scr[0:nc, :] = last_f
    hb_scr[r:r + nc, :] = last_b
    hff_ref[...] = last_f
    hfb_ref[...] = last_b


def _s5_scan(u_tm, nc, wts_f, wts_b, h0f, h0b):
    rows = u_tm.shape[0]
    tc = S5_ROWS // nc
    n_chunks = rows // S5_ROWS
    wbf, wcf, af = wts_f
    wbb, wcb, ab = wts_b
    af = jnp.broadcast_to(af, (nc, 2 * SSM_S))
    ab = jnp.broadcast_to(ab, (nc, 2 * SSM_S))
    const3 = lambda c: (0, 0, 0)
    const2 = lambda c: (0, 0)
    return pl.pallas_call(
        functools.partial(_s5_kernel, nc=nc, tc=tc),
        out_shape=(jax.ShapeDtypeStruct((rows, SSM_CH), F32), jax.ShapeDtypeStruct((rows, SSM_CH), F32),
                   jax.ShapeDtypeStruct((nc, 2 * SSM_S), F32), jax.ShapeDtypeStruct((nc, 2 * SSM_S), F32)),
        grid=(n_chunks,),
        in_specs=[pl.BlockSpec((S5_ROWS, SSM_CH), lambda c: (c, 0)),
                  pl.BlockSpec((S5_ROWS, SSM_CH), lambda c: (n_chunks - 1 - c, 0)),
                  pl.BlockSpec(wbf.shape, const3), pl.BlockSpec(wbb.shape, const3),
                  pl.BlockSpec(wcf.shape, const3), pl.BlockSpec(wcb.shape, const3),
                  pl.BlockSpec((nc, 2 * SSM_S), const2), pl.BlockSpec((nc, 2 * SSM_S), const2),
                  pl.BlockSpec((nc, 2 * SSM_S), const2), pl.BlockSpec((nc, 2 * SSM_S), const2)],
        out_specs=(pl.BlockSpec((S5_ROWS, SSM_CH), lambda c: (c, 0)),
                   pl.BlockSpec((S5_ROWS, SSM_CH), lambda c: (n_chunks - 1 - c, 0)),
                   pl.BlockSpec((nc, 2 * SSM_S), const2), pl.BlockSpec((nc, 2 * SSM_S), const2)),
        scratch_shapes=[pltpu.VMEM((S5_ROWS + nc, 2 * SSM_S), F32), pltpu.VMEM((S5_ROWS + nc, 2 * SSM_S), F32)],
        compiler_params=_params(1, VMEM_BIG),
        name="s5_scan",
    )(u_tm, u_tm, wbf, wbb, wcf, wcb, af, ab, h0f, h0b)


def _ssm_glu_kernel(u_ref, y_ref, d_ref, w_ref, o_ref):
    y = jax.nn.gelu(u_ref[...] * d_ref[...] + y_ref[...])
    o_ref[...] = (y * jax.nn.sigmoid(jnp.dot(y.astype(BF), w_ref[...], preferred_element_type=F32))).astype(BF)


def _ssm_glu(z, y_scan, d, w_glu):
    tm = 1024
    return pl.pallas_call(
        _ssm_glu_kernel,
        out_shape=jax.ShapeDtypeStruct((T_ALL, SSM_CH), BF),
        grid=(T_ALL // tm,),
        in_specs=[pl.BlockSpec((tm, SSM_CH), lambda i: (i, COL_U)),
                  pl.BlockSpec((tm, SSM_CH), lambda i: (i, 0)),
                  pl.BlockSpec((1, SSM_CH), lambda i: (0, 0)),
                  pl.BlockSpec((SSM_CH, SSM_CH), lambda i: (0, 0))],
        out_specs=pl.BlockSpec((tm, SSM_CH), lambda i: (i, 0)),
        compiler_params=_params(1),
        name="ssm_glu",
    )(z, y_scan, d.reshape(1, SSM_CH), w_glu)


CONV_PAD = 16
CONV_ROWS = 64


def _conv_kernel(a_ref, g_ref, w_ref, b_ref, lg_ref, lb_ref, o_ref, z_scr, *, seq):
    z_scr[0:CONV_PAD, :] = jnp.zeros((CONV_PAD, CONV_CH), F32)
    z_scr[CONV_PAD + seq:2 * CONV_PAD + seq, :] = jnp.zeros((CONV_PAD, CONV_CH), F32)
    z_scr[CONV_PAD:CONV_PAD + seq, :] = a_ref[...] * jax.nn.sigmoid(g_ref[...])
    off = CONV_PAD - CONV_K // 2

    def tile(i, carry):
        base = pl.multiple_of(i * CONV_ROWS, CONV_ROWS)
        win = z_scr[pl.ds(base, CONV_ROWS + 2 * CONV_PAD), :]
        acc = jnp.zeros((CONV_ROWS, CONV_CH), F32)
        for k in range(CONV_K):
            acc = acc + win[off + k:off + k + CONV_ROWS, :] * w_ref[k:k + 1, :]
        acc = acc + b_ref[...]
        mu = jnp.mean(acc, axis=-1, keepdims=True)
        cen = acc - mu
        var = jnp.mean(cen * cen, axis=-1, keepdims=True)
        y = cen * lax.rsqrt(var + EPS) * lg_ref[...] + lb_ref[...]
        o_ref[pl.ds(base, CONV_ROWS), :] = _silu(y).astype(BF)
        return carry

    lax.fori_loop(0, seq // CONV_ROWS, tile, 0)


def _conv_module(z, seq, row_block0, n_seq, w, b, lg, lb):
    row = lambda v: v.reshape(1, CONV_CH)
    wpad = jnp.pad(w, ((0, 32 - CONV_K), (0, 0)))
    vec = pl.BlockSpec((1, CONV_CH), lambda s: (0, 0))
    return pl.pallas_call(
        functools.partial(_conv_kernel, seq=seq),
        out_shape=jax.ShapeDtypeStruct((n_seq * seq, CONV_CH), BF),
        grid=(n_seq,),
        in_specs=[pl.BlockSpec((seq, CONV_CH), lambda s: (row_block0 + s, COL_CA)),
                  pl.BlockSpec((seq, CONV_CH), lambda s: (row_block0 + s, COL_CG)),
                  pl.BlockSpec((32, CONV_CH), lambda s: (0, 0)), vec, vec, vec],
        out_specs=pl.BlockSpec((seq, CONV_CH), lambda s: (s, 0)),
        scratch_shapes=[pltpu.VMEM((seq + 2 * CONV_PAD, CONV_CH), F32)],
        compiler_params=_params(1, VMEM_BIG),
        name="conv_module",
    )(z, z, wpad, row(b), row(lg), row(lb))


def _merge_kernel(h_ref, o_ref, wg_ref, wb_ref, y_ref, acc_ref):
    j = pl.program_id(2)
    gate = jax.nn.sigmoid(jnp.dot(h_ref[...], wg_ref[...], preferred_element_type=F32))
    t = gate * jnp.dot(o_ref[0], wb_ref[0], preferred_element_type=F32)

    @pl.when(j == 0)
    def _():
        acc_ref[...] = t

    @pl.when(j > 0)
    def _():
        acc_ref[...] += t

    @pl.when(j == N_BRANCH - 1)
    def _():
        y_ref[...] = acc_ref[...].astype(BF)


def _merge(h, branches, w_gate, w_br):
    tm, tn = 512, 1024
    nct = D_MODEL // tn
    return pl.pallas_call(
        _merge_kernel,
        out_shape=jax.ShapeDtypeStruct((T_ALL, D_MODEL), BF),
        grid=(T_ALL // tm, nct, N_BRANCH),
        in_specs=[pl.BlockSpec((tm, D_MODEL), lambda i, c, j: (i, 0)),
                  pl.BlockSpec((1, tm, 512), lambda i, c, j: (j, i, 0)),
                  pl.BlockSpec((D_MODEL, tn), lambda i, c, j: (0, j * nct + c)),
                  pl.BlockSpec((1, 512, tn), lambda i, c, j: (j, 0, c))],
        out_specs=pl.BlockSpec((tm, tn), lambda i, c, j: (i, c)),
        scratch_shapes=[pltpu.VMEM((tm, tn), F32)],
        compiler_params=_params(3, VMEM_BIG),
        name="merge",
    )(h, branches, w_gate, w_br)


def _pack_halves(v):
    k = v.shape[1] // 2
    bits = lax.bitcast_convert_type(v.astype(BF).astype(F32), jnp.uint32)
    return (bits[:, :k] >> 16) | bits[:, k:]


def _unpack_halves(w):
    lo = lax.bitcast_convert_type(w << 16, F32)
    hi = lax.bitcast_convert_type(w & jnp.uint32(0xFFFF0000), F32)
    return lo, hi


def _to_tile_rows(v, ref):
    for s in range(8):
        ref[:, s, :] = v[:, LANES * s:LANES * (s + 1)]


def _from_tile_rows(ref):
    return jnp.concatenate([ref[:, s, :] for s in range(8)], axis=1)


def _out_router_kernel(y_ref, w_ref, x_ref, mod_ref, g_ref, rh_ref, rl_ref, rb_ref,
                       xo_ref, h_ref, hp_ref, idx_ref, wts_ref, rank_ref, cnt_ref):
    x = x_ref[...] + mod_ref[0, 2:3, :] * jnp.dot(y_ref[...], w_ref[...], preferred_element_type=F32)
    xo_ref[...] = x
    n = x * lax.rsqrt(jnp.mean(x * x, axis=-1, keepdims=True) + EPS) * g_ref[...]
    h = n * (1.0 + mod_ref[0, 4:5, :]) + mod_ref[0, 3:4, :]
    hb = h.astype(BF)
    h_ref[...] = hb
    _to_tile_rows(_pack_halves(h), hp_ref)
    h_lo = (h - hb.astype(F32)).astype(BF)
    logits = (jnp.dot(hb, rh_ref[...], preferred_element_type=F32)
              + jnp.dot(h_lo, rh_ref[...], preferred_element_type=F32)
              + jnp.dot(hb, rl_ref[...], preferred_element_type=F32))
    scores = jax.nn.sigmoid(logits)
    lane = lax.broadcasted_iota(jnp.int32, scores.shape, 1)
    cand = jnp.where(lane < N_EXPERTS, scores + rb_ref[...], -jnp.inf)
    idx = jnp.zeros(scores.shape, jnp.int32)
    wts = jnp.zeros(scores.shape, F32)
    total = jnp.zeros((scores.shape[0], 1), F32)
    member = jnp.zeros(scores.shape, F32)
    firsts = []
    for k in range(TOP_K):
        m = jnp.max(cand, axis=-1, keepdims=True)
        first = jnp.min(jnp.where(cand == m, lane, LANES), axis=-1, keepdims=True)
        hit = lane == first
        s_k = jnp.sum(jnp.where(hit, scores, 0.0), axis=-1, keepdims=True)
        idx = jnp.where(lane == k, first, idx)
        wts = jnp.where(lane == k, s_k, wts)
        total = total + s_k
        member = jnp.where(hit, 1.0, member)
        cand = jnp.where(hit, -jnp.inf, cand)
        firsts.append(first)
    idx_ref[...] = idx
    wts_ref[...] = wts / total * ROUTE_SCALE

    @pl.when(pl.program_id(0) == 0)
    def _():
        cnt_ref[...] = jnp.zeros(cnt_ref.shape, F32)

    tm = scores.shape[0]
    earlier = (lax.broadcasted_iota(jnp.int32, (tm, tm), 1) < lax.broadcasted_iota(jnp.int32, (tm, tm), 0))
    prior = jnp.dot(earlier.astype(BF), member.astype(BF), preferred_element_type=F32) + cnt_ref[0:1, :]
    rank = jnp.zeros(scores.shape, F32)
    for k in range(TOP_K):
        r_k = jnp.sum(jnp.where(lane == firsts[k], prior, 0.0), axis=-1, keepdims=True)
        rank = jnp.where(lane == k, r_k, rank)
    rank_ref[...] = rank.astype(jnp.int32)
    cnt_ref[...] = cnt_ref[...] + jnp.sum(member, axis=0, keepdims=True)


def _out_router(y, w_out, x, modt, g2, r_hi, r_lo, r_bias):
    tm = 512
    full = lambda i: (0, 0)
    rows = lambda width: pl.BlockSpec((tm, width), lambda i: (i, 0))
    return pl.pallas_call(
        _out_router_kernel,
        out_shape=(jax.ShapeDtypeStruct((T_ALL, D_MODEL), F32), jax.ShapeDtypeStruct((T_ALL, D_MODEL), BF),
                   jax.ShapeDtypeStruct((T_ALL, 8, LANES), jnp.uint32),
                   jax.ShapeDtypeStruct((T_ALL, LANES), jnp.int32), jax.ShapeDtypeStruct((T_ALL, LANES), F32),
                   jax.ShapeDtypeStruct((T_ALL, LANES), jnp.int32), jax.ShapeDtypeStruct((8, LANES), F32)),
        grid=(T_ALL // tm,),
        in_specs=[rows(D_MODEL),
                  pl.BlockSpec((D_MODEL, D_MODEL), full),
                  rows(D_MODEL),
                  pl.BlockSpec((1, 8, D_MODEL), lambda i: (i, 0, 0)),
                  pl.BlockSpec((1, D_MODEL), full),
                  pl.BlockSpec((D_MODEL, LANES), full), pl.BlockSpec((D_MODEL, LANES), full),
                  pl.BlockSpec((1, LANES), full)],
        out_specs=(rows(D_MODEL), rows(D_MODEL), pl.BlockSpec((tm, 8, LANES), lambda i: (i, 0, 0)),
                   rows(LANES), rows(LANES), rows(LANES), pl.BlockSpec((8, LANES), full)),
        compiler_params=_params(1, VMEM_BIG),
        name="out_router",
    )(y, w_out, x, modt, g2.reshape(1, D_MODEL), r_hi, r_lo, r_bias)


MOE_TILE = 256
MOE_PAIRS = T_ALL * TOP_K
MOE_SLOTS = -(-(MOE_PAIRS + N_EXPERTS * (MOE_TILE - 1)) // MOE_TILE) * MOE_TILE
MOE_NT = MOE_SLOTS // MOE_TILE
MOE_SPARE = MOE_SLOTS - MOE_PAIRS
COPY_CHUNK = 1024


def _route_plan(idx, rank, counts):
    i32 = jnp.int32
    experts = jnp.arange(N_EXPERTS, dtype=i32)
    cnt = counts[0, :N_EXPERTS].astype(i32)
    padded = (cnt + MOE_TILE - 1) // MOE_TILE * MOE_TILE
    upto = experts[:, None] >= experts[None, :]
    ends = jnp.sum(jnp.where(upto, padded[None, :], 0), axis=1)
    starts = ends - padded
    total = ends[-1]
    onehot = idx[:, :TOP_K, None] == experts
    slot = jnp.sum(jnp.where(onehot, starts, 0), axis=-1) + rank[:, :TOP_K]
    slot = slot.T.reshape(-1)
    n_pad = padded - cnt
    pad_ends = jnp.sum(jnp.where(upto, n_pad[None, :], 0), axis=1)
    q = jnp.arange(MOE_SPARE, dtype=i32)
    group = jnp.sum((pad_ends[None, :] <= q[:, None]).astype(i32), axis=1)
    pad_base = starts + cnt - (pad_ends - n_pad)
    in_group = jnp.sum(jnp.where(group[:, None] == experts, pad_base, 0), axis=1) + q
    spare = jnp.where(group < N_EXPERTS, in_group, total + q - pad_ends[-1])
    src = jnp.concatenate([jnp.arange(MOE_PAIRS, dtype=i32) % T_ALL, jnp.zeros((MOE_SPARE,), i32)])
    dst = jnp.concatenate([slot, spare])
    tile_start = jnp.arange(MOE_NT, dtype=i32) * MOE_TILE
    tile_e = jnp.sum((starts[None, :] <= tile_start[:, None]).astype(i32), axis=1) - 1
    tile_e = jnp.clip(tile_e, 0, N_EXPERTS - 1)
    tile_v = (tile_start < total).astype(i32)
    return src, dst, slot, tile_e, tile_v


def _row_copy_kernel(src_ref, dst_ref, x_hbm, o_hbm, sem):
    i = pl.program_id(0)
    n = pl.num_programs(0)

    def copy(s, d, j):
        return pltpu.make_async_copy(x_hbm.at[s], o_hbm.at[d], sem.at[j])

    for r in range(COPY_CHUNK):
        copy(src_ref[0, 0, r], dst_ref[0, 0, r], i % 2).start()

    @pl.when(i > 0)
    def _():
        for r in range(COPY_CHUNK):
            copy(0, 0, 1 - i % 2).wait()

    @pl.when(i == n - 1)
    def _():
        for r in range(COPY_CHUNK):
            copy(0, 0, i % 2).wait()


def _row_copy(x, src, dst, n_out, name):
    steps = src.shape[0] // COPY_CHUNK
    chunk = lambda: pl.BlockSpec((1, 1, COPY_CHUNK), lambda i: (i, 0, 0), memory_space=pltpu.SMEM)
    return pl.pallas_call(
        _row_copy_kernel,
        out_shape=jax.ShapeDtypeStruct((n_out,) + x.shape[1:], x.dtype),
        grid_spec=pltpu.PrefetchScalarGridSpec(
            num_scalar_prefetch=0, grid=(steps,),
            in_specs=[chunk(), chunk(), pl.BlockSpec(memory_space=pl.ANY)],
            out_specs=pl.BlockSpec(memory_space=pl.ANY),
            scratch_shapes=[pltpu.SemaphoreType.DMA((2,))]),
        compiler_params=_params(1),
        name=name,
    )(src.reshape(steps, 1, COPY_CHUNK), dst.reshape(steps, 1, COPY_CHUNK), x)


def _gmm_kernel(te_ref, tv_ref, x_ref, w1_ref, w3_ref, w2_ref, y_ref, w13_scr, w2_scr):
    i = pl.program_id(0)

    @pl.when(tv_ref[i] == 1)
    def _():
        @pl.when(jnp.logical_or(i == 0, te_ref[i] != te_ref[jnp.maximum(i - 1, 0)]))
        def _():
            w13_scr[:, :EXPERT_F] = w1_ref[0, 0].astype(BF)
            w13_scr[:, EXPERT_F:] = w3_ref[0, 0].astype(BF)
            w2_scr[...] = w2_ref[0, 0].astype(BF)

        lo, hi = _unpack_halves(_from_tile_rows(x_ref))
        half = D_MODEL // 2
        ab = (jnp.dot(lo.astype(BF), w13_scr[:half, :], preferred_element_type=F32)
              + jnp.dot(hi.astype(BF), w13_scr[half:, :], preferred_element_type=F32))
        zed = (_silu(ab[:, :EXPERT_F]) * ab[:, EXPERT_F:]).astype(BF)
        _to_tile_rows(_pack_halves(jnp.dot(zed, w2_scr[...], preferred_element_type=F32)), y_ref)

    @pl.when(tv_ref[i] == 0)
    def _():
        y_ref[...] = jnp.zeros(y_ref.shape, y_ref.dtype)


def _grouped_experts(xs, tile_e, tile_v, w1, w3, w2, layer):
    wspec = lambda shape: pl.BlockSpec((1, 1) + shape, lambda i, te, tv: (layer, te[i], 0, 0))
    tile = pl.BlockSpec((MOE_TILE, 8, LANES), lambda i, te, tv: (i, 0, 0))
    grid_spec = pltpu.PrefetchScalarGridSpec(
        num_scalar_prefetch=2,
        grid=(MOE_NT,),
        in_specs=[tile, wspec((D_MODEL, EXPERT_F)), wspec((D_MODEL, EXPERT_F)), wspec((EXPERT_F, D_MODEL))],
        out_specs=tile,
        scratch_shapes=[pltpu.VMEM((D_MODEL, 2 * EXPERT_F), BF), pltpu.VMEM((EXPERT_F, D_MODEL), BF)])
    return pl.pallas_call(
        _gmm_kernel,
        out_shape=jax.ShapeDtypeStruct(xs.shape, jnp.uint32),
        grid_spec=grid_spec,
        compiler_params=_params(1, VMEM_BIG),
        name="grouped_experts",
    )(tile_e, tile_v, xs, w1, w3, w2)


def _combine_kernel(x_ref, h_ref, wts_ref, mod_ref, w1_ref, w3_ref, w2_ref, *rest):
    y_refs, o_ref = rest[:TOP_K], rest[TOP_K]
    h = h_ref[...]
    a = jnp.dot(h, w1_ref[0].astype(BF), preferred_element_type=F32)
    b = jnp.dot(h, w3_ref[0].astype(BF), preferred_element_type=F32)
    acc = jnp.dot((_silu(a) * b).astype(BF), w2_ref[0].astype(BF), preferred_element_type=F32)
    half = D_MODEL // 2
    acc_lo, acc_hi = acc[:, :half], acc[:, half:]
    wts = wts_ref[...]
    for k in range(TOP_K):
        lo, hi = _unpack_halves(_from_tile_rows(y_refs[k]))
        w_k = wts[:, k:k + 1]
        acc_lo = acc_lo + w_k * lo
        acc_hi = acc_hi + w_k * hi
    gate = mod_ref[0, 5:6, :]
    o_ref[:, :half] = x_ref[:, :half] + gate[:, :half] * acc_lo
    o_ref[:, half:] = x_ref[:, half:] + gate[:, half:] * acc_hi


def _combine(x, h, wts, modt, sw1, sw3, sw2, y, layer):
    tm = 256
    nt = T_ALL // tm
    rows = lambda width: pl.BlockSpec((tm, width), lambda i: (i, 0))
    y_specs = [pl.BlockSpec((tm, 8, LANES), functools.partial(lambda i, k: (k * nt + i, 0, 0), k=k))
               for k in range(TOP_K)]
    return pl.pallas_call(
        _combine_kernel,
        out_shape=jax.ShapeDtypeStruct((T_ALL, D_MODEL), F32),
        grid=(nt,),
        in_specs=[rows(D_MODEL), rows(D_MODEL), rows(LANES),
                  pl.BlockSpec((1, 8, D_MODEL), lambda i: (i, 0, 0)),
                  pl.BlockSpec((1, D_MODEL, EXPERT_F), lambda i: (layer, 0, 0)),
                  pl.BlockSpec((1, D_MODEL, EXPERT_F), lambda i: (layer, 0, 0)),
                  pl.BlockSpec((1, EXPERT_F, D_MODEL), lambda i: (layer, 0, 0))] + y_specs,
        out_specs=rows(D_MODEL),
        compiler_params=_params(1, VMEM_BIG),
        name="moe_combine",
    )(x, h, wts, modt, sw1, sw3, sw2, *([y] * TOP_K))


def _final_kernel(x_ref, g_ref, o_ref):
    x = x_ref[...]
    o_ref[...] = x * lax.rsqrt(jnp.mean(x * x, axis=-1, keepdims=True) + EPS) * g_ref[...]


def _final_norm(x, g):
    tm = 512
    return pl.pallas_call(
        _final_kernel,
        out_shape=jax.ShapeDtypeStruct((T_ALL, D_MODEL), F32),
        grid=(T_ALL // tm,),
        in_specs=[pl.BlockSpec((tm, D_MODEL), lambda i: (i, 0)), pl.BlockSpec((1, D_MODEL), lambda i: (0, 0))],
        out_specs=pl.BlockSpec((tm, D_MODEL), lambda i: (i, 0)),
        compiler_params=_params(1),
        name="final_norm",
    )(x, g.reshape(1, D_MODEL))


def _time_major(u, n_chain):
    b, l, c = u.shape
    u = jnp.transpose(u, (1, 0, 2))
    u = jnp.pad(u, ((0, 0), (0, n_chain - b), (0, 0)))
    return u.reshape(l * n_chain, c)


def _from_time_major(y, b, l, n_chain):
    return jnp.transpose(y.reshape(l, n_chain, -1)[:, :b], (1, 0, 2)).reshape(b * l, -1)


def _diff_lambda(lam_q, lam_k, lam_init):
    lq = lam_q.astype(F32)
    lk = lam_k.astype(F32)
    return jnp.exp(jnp.sum(lq[0] * lk[0])) - jnp.exp(jnp.sum(lq[1] * lk[1])) + lam_init


def kernel(x_prompt, x_sample, cache_na_k, cache_na_v, cache_da_k, cache_da_v, state_ssm, c, c_ctx, ada_w, ada_b, norm1_g, norm2_g, w_in, w_gate, na_rpb, da_lam_q, da_lam_k, da_subln_g, ssm_lam_re, ssm_lam_im, ssm_log_dt, ssm_b_re, ssm_b_im, ssm_c_re, ssm_c_im, ssm_d, ssm_w_glu, conv_w, conv_b, conv_ln_g, conv_ln_b, w_br_na, w_br_da, w_br_ssm, w_br_conv, w_out, moe_router, moe_bias, moe_w1, moe_w3, moe_w2, shared_w1, shared_w3, shared_w2, final_g):
    x = jnp.concatenate([x_prompt.reshape(T_CTX, D_MODEL), x_sample.reshape(T_LAT, D_MODEL)], axis=0)
    cvec = jnp.zeros((8, D_MODEL), F32).at[0].set(c_ctx).at[1:1 + DEC_BATCH].set(c)
    mods_all = _adaln(cvec, ada_w, ada_b)
    cos, sin = _rope_tables()
    lat_chains = 8

    na_ks, na_vs, da_ks, da_vs, ssm_states = [], [], [], [], []
    for layer in range(DEPTH):
        mods = mods_all[layer, :1 + DEC_BATCH].reshape(1 + DEC_BATCH, 6, D_MODEL)
        modt = _mod_table(mods, 512)
        h1, z = _in_proj(x, modt, norm1_g[layer], w_in[layer].astype(BF))

        zc = z[:T_CTX]
        na_ks.append(zc[:, 512 * COL_NAK:512 * (COL_NAK + 1)].reshape(BATCH, SEQ, NA_HEADS, NA_HD))
        na_vs.append(zc[:, 512 * COL_NAV:512 * (COL_NAV + 1)].reshape(BATCH, SEQ, NA_HEADS, NA_HD))
        da_ks.append(zc[:, 512 * COL_DAK:512 * (COL_DAK + 1)].reshape(BATCH, SEQ, DA_HEADS, 2, DA_QK))
        da_vs.append(zc[:, 512 * COL_DAV:512 * (COL_DAV + 1)].reshape(BATCH, SEQ, DA_HEADS, DA_VD))

        lam_init = 0.8 - 0.6 * math.exp(-0.3 * layer)
        lam = _diff_lambda(da_lam_q[layer], da_lam_k[layer], lam_init).reshape(1, 1)
        subg = da_subln_g[layer].reshape(1, DA_VD)
        ona_c, oda_c = _attn_ctx(z, lam, subg, 1.0 - lam_init)
        ona_l = _na_lat(z, cache_na_k[:, layer].reshape(DEC_BATCH, PAST_LEN, 512),
                        cache_na_v[:, layer].reshape(DEC_BATCH, PAST_LEN, 512), _na_bias_table(na_rpb[layer]))
        qr, kr = _rope_lat(z, cos, sin)
        oda_l = _da_lat(qr, kr, z, cache_da_k[:, layer].reshape(DEC_BATCH, PAST_LEN, 512),
                        cache_da_v[:, layer].reshape(DEC_BATCH, PAST_LEN, 512), lam, subg, 1.0 - lam_init)

        wts = []
        for d in range(2):
            ar, ai, bbr, bbi = _s5_discretize(ssm_lam_re[layer, d], ssm_lam_im[layer, d], ssm_log_dt[layer, d],
                                              ssm_b_re[layer, d], ssm_b_im[layer, d])
            wts.append(_s5_weights(ar, ai, bbr, bbi, ssm_c_re[layer, d], ssm_c_im[layer, d]))
        u = z[:, 512 * COL_U:512 * (COL_U + 1)].astype(BF)
        zeros_c = jnp.zeros((BATCH, 2 * SSM_S), F32)
        yf, yb, hf, hb = _s5_scan(_time_major(u[:T_CTX].reshape(BATCH, SEQ, SSM_CH), BATCH), BATCH,
                                  wts[0], wts[1], zeros_c, zeros_c)
        y_ctx = _from_time_major(yf + yb, BATCH, SEQ, BATCH)
        ssm_states.append(jnp.stack([hf.reshape(BATCH, 2, SSM_G, SSM_N), hb.reshape(BATCH, 2, SSM_G, SSM_N)], axis=1))
        h0 = state_ssm[:, layer].astype(F32).reshape(DEC_BATCH, 2, 2 * SSM_S)
        h0 = jnp.pad(h0, ((0, lat_chains - DEC_BATCH), (0, 0), (0, 0)))
        yf, yb, _, _ = _s5_scan(_time_major(u[T_CTX:].reshape(DEC_BATCH, DEC_SEQ, SSM_CH), lat_chains), lat_chains,
                                wts[0], wts[1], h0[:, 0], h0[:, 1])
        y_lat = _from_time_major(yf + yb, DEC_BATCH, DEC_SEQ, lat_chains)
        o_ssm = _ssm_glu(z, jnp.concatenate([y_ctx, y_lat], axis=0), ssm_d[layer], ssm_w_glu[layer].astype(BF))

        conv_args = (conv_w[layer], conv_b[layer], conv_ln_g[layer], conv_ln_b[layer])
        o_conv = jnp.concatenate([_conv_module(z, SEQ, 0, BATCH, *conv_args),
                                  _conv_module(z, DEC_SEQ, T_CTX // DEC_SEQ, DEC_BATCH, *conv_args)], axis=0)

        branches = jnp.stack([jnp.concatenate([ona_c, ona_l], axis=0), jnp.concatenate([oda_c, oda_l], axis=0),
                              o_ssm, o_conv], axis=0)
        w_br = jnp.stack([w_br_na[layer], w_br_da[layer], w_br_ssm[layer], w_br_conv[layer]], axis=0).astype(BF)
        y = _merge(h1, branches, w_gate[layer].astype(BF), w_br)

        router = jnp.pad(moe_router[layer], ((0, 0), (0, LANES - N_EXPERTS)))
        r_hi = router.astype(BF)
        r_lo = (router - r_hi.astype(F32)).astype(BF)
        r_bias = jnp.pad(moe_bias[layer].astype(F32), (0, LANES - N_EXPERTS)).reshape(1, LANES)
        x_mid, h2, h2p, idx, wts, rank, counts = _out_router(y, w_out[layer].astype(BF), x, modt, norm2_g[layer],
                                                             r_hi, r_lo, r_bias)
        src, dst, slot, tile_e, tile_v = _route_plan(idx, rank, counts)
        xs = _row_copy(h2p, src, dst, MOE_SLOTS, "moe_dispatch")
        ys = _grouped_experts(xs, tile_e, tile_v, moe_w1, moe_w3, moe_w2, layer)
        y_pairs = _row_copy(ys, slot, jnp.arange(MOE_PAIRS, dtype=jnp.int32), MOE_PAIRS, "moe_collect")
        x = _combine(x_mid, h2, wts, _mod_table(mods, 256), shared_w1, shared_w3, shared_w2, y_pairs, layer)

    y = _final_norm(x, final_g)
    y_prompt = y[:T_CTX].reshape(BATCH, SEQ, D_MODEL)
    y_sample = y[T_CTX:].reshape(DEC_BATCH, DEC_SEQ, D_MODEL)
    return (y_prompt, y_sample, jnp.stack(na_ks, axis=1), jnp.stack(na_vs, axis=1), jnp.stack(da_ks, axis=1),
            jnp.stack(da_vs, axis=1), jnp.stack(ssm_states, axis=1))
```

```python
import functools
import math

import numpy as np
import jax
import jax.numpy as jnp
from jax import lax
from jax.experimental import pallas as pl
from jax.experimental.pallas import tpu as pltpu

F32 = jnp.float32
BF = jnp.bfloat16

D_MODEL = 2048
BATCH = 32
SEQ = 256
DEPTH = 2
DEC_BATCH = 2
DEC_SEQ = 2048
PAST_LEN = 256
GRID_W = 64
NA_HEADS = 8
NA_HD = 64
NA_KH = 8
NA_KW = 16
NA_W = NA_HEADS * NA_HD
DA_HEADS = 4
DA_QK = 64
DA_VD = 2 * DA_QK
SSM_CH = 512
SSM_P = 16
SSM_G = SSM_CH // SSM_P
SSM_N = 64
SSM_S = SSM_G * SSM_N
CONV_CH = 512
CONV_K = 31
IN_W = 4608
N_BRANCH = 4
N_EXPERTS = 64
TOP_K = 8
EXPERT_F = 256
ROUTE_SCALE = 2.5
ROPE_BASE = 10000.0
EPS = 1e-6
NEG_INF = -1e30

T_CTX = BATCH * SEQ
T_LAT = DEC_BATCH * DEC_SEQ
T_ALL = T_CTX + T_LAT

COL_NAQ, COL_NAK, COL_NAV, COL_DAQ, COL_DAK, COL_DAV, COL_U, COL_CA, COL_CG = range(9)

LANES = 128
ROWS_NA_Q = 4
ROWS_NA_K = 12
S5_ROWS = 512
VMEM_BIG = 56 * 1024 * 1024


def _params(n_axes, vmem=None):
    return pltpu.CompilerParams(dimension_semantics=("arbitrary",) * n_axes, vmem_limit_bytes=vmem)


def _silu(x):
    return x * jax.nn.sigmoid(x)


def _ada_kernel(c_ref, w_ref, b_ref, o_ref):
    c = c_ref[...]
    o_ref[0] = jnp.dot(_silu(c), w_ref[0], preferred_element_type=F32,
                       precision=lax.Precision.HIGHEST) + b_ref[0]


def _adaln(cvec, ada_w, ada_b):
    tn = 1536
    n = 6 * D_MODEL
    return pl.pallas_call(
        _ada_kernel,
        out_shape=jax.ShapeDtypeStruct((DEPTH, 8, n), F32),
        grid=(DEPTH, n // tn),
        in_specs=[pl.BlockSpec((8, D_MODEL), lambda l, j: (0, 0)),
                  pl.BlockSpec((1, D_MODEL, tn), lambda l, j: (l, 0, j)),
                  pl.BlockSpec((1, 1, tn), lambda l, j: (l, 0, j))],
        out_specs=pl.BlockSpec((1, 8, tn), lambda l, j: (l, 0, j)),
        compiler_params=_params(2, VMEM_BIG),
        name="adaln",
    )(cvec, ada_w, ada_b.reshape(DEPTH, 1, n))


def _mod_table(mods, tm):
    starts = np.arange(0, T_ALL, tm)
    idx = np.where(starts < T_CTX, 0, 1 + (starts - T_CTX) // DEC_SEQ)
    tab = mods[idx]
    return jnp.pad(tab, ((0, 0), (0, 2), (0, 0)))


IN_TN = 1536


def _in_kernel(x_ref, mod_ref, g_ref, w_ref, h_ref, z_ref, u_ref):
    j = pl.program_id(1)

    @pl.when(j == 0)
    def _():
        x = x_ref[...]
        y = x * lax.rsqrt(jnp.mean(x * x, axis=-1, keepdims=True) + EPS) * g_ref[...]
        h = y * (1.0 + mod_ref[0, 1:2, :]) + mod_ref[0, 0:1, :]
        h_ref[...] = h.astype(BF)

    z = jnp.dot(h_ref[...], w_ref[...], preferred_element_type=F32)
    z_ref[...] = z

    @pl.when(j == (512 * COL_U) // IN_TN)
    def _():
        u_ref[...] = z[:, :SSM_CH].astype(BF)


def _in_proj(x, modt, g, w):
    tm, tn = 512, IN_TN
    assert (512 * COL_U) % tn == 0
    t = x.shape[0]
    return pl.pallas_call(
        _in_kernel,
        out_shape=(jax.ShapeDtypeStruct((t, D_MODEL), BF), jax.ShapeDtypeStruct((t, IN_W), F32),
                   jax.ShapeDtypeStruct((t, SSM_CH), BF)),
        grid=(t // tm, IN_W // tn),
        in_specs=[pl.BlockSpec((tm, D_MODEL), lambda i, j: (i, 0)),
                  pl.BlockSpec((1, 8, D_MODEL), lambda i, j: (i, 0, 0)),
                  pl.BlockSpec((1, D_MODEL), lambda i, j: (0, 0)),
                  pl.BlockSpec((D_MODEL, tn), lambda i, j: (0, j))],
        out_specs=(pl.BlockSpec((tm, D_MODEL), lambda i, j: (i, 0)),
                   pl.BlockSpec((tm, tn), lambda i, j: (i, j)),
                   pl.BlockSpec((tm, SSM_CH), lambda i, j: (i, 0))),
        compiler_params=_params(2, VMEM_BIG),
        name="in_proj",
    )(x, modt, g.reshape(1, D_MODEL), w)


def _half_masks():
    lane = lax.broadcasted_iota(jnp.int32, (1, LANES), 1)
    lo = lane < (LANES // 2)
    return lo, jnp.logical_not(lo)


def _nt_dot(a, b):
    return lax.dot_general(a, b, (((1,), (1,)), ((), ())), preferred_element_type=F32)


def _softmax_parts(parts):
    m = jnp.max(parts[0], axis=-1, keepdims=True)
    for s in parts[1:]:
        m = jnp.maximum(m, jnp.max(s, axis=-1, keepdims=True))
    ps = [jnp.exp(s - m) for s in parts]
    den = jnp.sum(ps[0], axis=-1, keepdims=True)
    for p in ps[1:]:
        den = den + jnp.sum(p, axis=-1, keepdims=True)
    return ps, 1.0 / den


def _na_heads(q, k, v, o_ref, kc=None, vc=None, bias_ref=None):
    masks = _half_masks()
    for pair in range(NA_HEADS // 2):
        sl = slice(LANES * pair, LANES * (pair + 1))
        q128, k128, v128 = q[:, sl], k[:, sl], v[:, sl]
        o_pair = None
        for half in range(2):
            qm = jnp.where(masks[half], q128, 0.0).astype(BF)
            s = _nt_dot(qm, k128)
            if bias_ref is not None:
                s = s + bias_ref[0, 2 * pair + half]
            parts = [s]
            if kc is not None:
                parts.append(_nt_dot(qm, kc[:, sl]))
            ps, inv = _softmax_parts(parts)
            o = jnp.dot(ps[0].astype(BF), v128, preferred_element_type=F32)
            if kc is not None:
                o = o + jnp.dot(ps[1].astype(BF), vc[:, sl], preferred_element_type=F32)
            o = o * inv
            o_pair = o if half == 0 else jnp.where(masks[0], o_pair, o)
        o_ref[:, sl] = o_pair.astype(o_ref.dtype)


def _da_heads(q, k, v, lam, subg, out_scale, o_ref, kc=None, vc=None):
    masks = _half_masks()
    for h in range(DA_HEADS):
        sl = slice(LANES * h, LANES * (h + 1))
        q128, k128, v128 = q[:, sl], k[:, sl], v[:, sl]
        probs = []
        for m in range(2):
            qm = jnp.where(masks[m], q128, 0.0).astype(BF)
            parts = [_nt_dot(qm, k128)]
            if kc is not None:
                parts.append(_nt_dot(qm, kc[:, sl]))
            probs.append(_softmax_parts(parts))
        (p0, inv0), (p1, inv1) = probs
        r1 = lam * inv1
        a = (p0[0] * inv0 - p1[0] * r1).astype(BF)
        o = jnp.dot(a, v128, preferred_element_type=F32)
        if kc is not None:
            ac = (p0[1] * inv0 - p1[1] * r1).astype(BF)
            o = o + jnp.dot(ac, vc[:, sl], preferred_element_type=F32)
        o = o * lax.rsqrt(jnp.mean(o * o, axis=-1, keepdims=True) + EPS) * subg * out_scale
        o_ref[:, sl] = o.astype(o_ref.dtype)


def _attn_ctx_kernel(lam_ref, naq_ref, nak_ref, nav_ref, daq_ref, dak_ref, dav_ref, subg_ref,
                     ona_ref, oda_ref, *, out_scale):
    _na_heads(naq_ref[...] * (NA_HD ** -0.5), nak_ref[...].astype(BF), nav_ref[...].astype(BF), ona_ref)
    _da_heads(daq_ref[...] * (DA_QK ** -0.5), dak_ref[...].astype(BF), dav_ref[...].astype(BF),
              lam_ref[0, 0], subg_ref[...], out_scale, oda_ref)


def _attn_ctx(z, lam, subg, out_scale):
    def col(c):
        return pl.BlockSpec((SEQ, 512), lambda b: (b, c))

    return pl.pallas_call(
        functools.partial(_attn_ctx_kernel, out_scale=out_scale),
        out_shape=(jax.ShapeDtypeStruct((T_CTX, 512), BF), jax.ShapeDtypeStruct((T_CTX, 512), BF)),
        grid=(BATCH,),
        in_specs=[pl.BlockSpec(memory_space=pltpu.SMEM),
                  col(COL_NAQ), col(COL_NAK), col(COL_NAV), col(COL_DAQ), col(COL_DAK), col(COL_DAV),
                  pl.BlockSpec((1, LANES), lambda b: (0, 0))],
        out_specs=(pl.BlockSpec((SEQ, 512), lambda b: (b, 0)), pl.BlockSpec((SEQ, 512), lambda b: (b, 0))),
        compiler_params=_params(1, VMEM_BIG),
        name="attn_ctx",
    )(lam, z, z, z, z, z, z, subg)


def _rope_tables():
    half = DA_QK // 2
    inv = ROPE_BASE ** (-jnp.arange(0, half, 2, dtype=F32) / half)
    t = jnp.arange(DEC_SEQ)
    pos = jnp.stack([t // GRID_W, t % GRID_W], axis=-1).astype(F32)
    ang = pos[:, :, None] * inv
    ang64 = jnp.broadcast_to(ang[:, :, None, :], (DEC_SEQ, 2, 2, half // 2)).reshape(DEC_SEQ, DA_QK)
    sign = jnp.tile(jnp.repeat(jnp.array([-1.0, 1.0], F32), half // 2), 2)
    cos = jnp.tile(jnp.cos(ang64), (1, 512 // DA_QK))
    sin = jnp.tile(jnp.sin(ang64) * sign, (1, 512 // DA_QK))
    return cos, sin


def _rope_kernel(q_ref, k_ref, cos_ref, sin_ref, qo_ref, ko_ref):
    lane = lax.broadcasted_iota(jnp.int32, (1, 512), 1)
    first = (lane % (DA_QK // 2)) < (DA_QK // 4)
    cos, sin = cos_ref[...], sin_ref[...]

    def rope(x):
        partner = jnp.where(first, pltpu.roll(x, 512 - DA_QK // 4, 1), pltpu.roll(x, DA_QK // 4, 1))
        return x * cos + partner * sin

    qo_ref[...] = (rope(q_ref[...]) * (DA_QK ** -0.5)).astype(BF)
    ko_ref[...] = rope(k_ref[...]).astype(BF)


def _rope_lat(z, cos, sin):
    tm = 512
    base = T_CTX // tm
    per_seq = DEC_SEQ // tm
    return pl.pallas_call(
        _rope_kernel,
        out_shape=(jax.ShapeDtypeStruct((T_LAT, 512), BF), jax.ShapeDtypeStruct((T_LAT, 512), BF)),
        grid=(T_LAT // tm,),
        in_specs=[pl.BlockSpec((tm, 512), lambda i: (base + i, COL_DAQ)),
                  pl.BlockSpec((tm, 512), lambda i: (base + i, COL_DAK)),
                  pl.BlockSpec((tm, 512), lambda i: (i % per_seq, 0)),
                  pl.BlockSpec((tm, 512), lambda i: (i % per_seq, 0))],
        out_specs=(pl.BlockSpec((tm, 512), lambda i: (i, 0)), pl.BlockSpec((tm, 512), lambda i: (i, 0))),
        compiler_params=_params(1),
        name="rope_lat",
    )(z, z, cos, sin)


def _da_lat_kernel(lam_ref, q_ref, k_ref, v_ref, kc_ref, vc_ref, subg_ref, o_ref, *, out_scale):
    _da_heads(q_ref[...].astype(F32), k_ref[...], v_ref[...].astype(BF), lam_ref[0, 0], subg_ref[...],
              out_scale, o_ref, kc=kc_ref[0].astype(BF), vc=vc_ref[0].astype(BF))


def _da_lat(qr, kr, z, kc, vc, lam, subg, out_scale):
    tq = 256
    nq = DEC_SEQ // tq
    zb = T_CTX // DEC_SEQ
    return pl.pallas_call(
        functools.partial(_da_lat_kernel, out_scale=out_scale),
        out_shape=jax.ShapeDtypeStruct((T_LAT, 512), BF),
        grid=(DEC_BATCH, nq),
        in_specs=[pl.BlockSpec(memory_space=pltpu.SMEM),
                  pl.BlockSpec((tq, 512), lambda b, i: (b * nq + i, 0)),
                  pl.BlockSpec((DEC_SEQ, 512), lambda b, i: (b, 0)),
                  pl.BlockSpec((DEC_SEQ, 512), lambda b, i: (zb + b, COL_DAV)),
                  pl.BlockSpec((1, PAST_LEN, 512), lambda b, i: (b, 0, 0)),
                  pl.BlockSpec((1, PAST_LEN, 512), lambda b, i: (b, 0, 0)),
                  pl.BlockSpec((1, LANES), lambda b, i: (0, 0))],
        out_specs=pl.BlockSpec((tq, 512), lambda b, i: (b * nq + i, 0)),
        compiler_params=_params(2, VMEM_BIG),
        name="da_lat",
    )(lam, qr, kr, z, kc, vc, subg)


def _na_bias_table(rpb):
    rows = DEC_SEQ // GRID_W
    qc = np.arange(GRID_W)[:, None]
    kc = np.arange(GRID_W)[None, :]
    win0 = np.clip(qc - NA_KW // 2, 0, GRID_W - NA_KW)
    col_ok = (kc >= win0) & (kc < win0 + NA_KW)
    d_col = np.clip(kc - qc, 1 - NA_KW, NA_KW - 1) + (NA_KW - 1)
    pick = (d_col[:, :, None] == np.arange(2 * NA_KW - 1)).astype(np.float32)
    slabs = jnp.einsum('hij,qkj->hiqk', rpb.astype(F32), pick, precision=lax.Precision.HIGHEST)
    slabs = jnp.where(col_ok[None, None], slabs, NEG_INF)
    masked = jnp.full((NA_HEADS, GRID_W, GRID_W), NEG_INF, F32)
    variants = []
    for r_first, r0 in ((0, 0), (ROWS_NA_Q, 0), (rows - ROWS_NA_Q, rows - ROWS_NA_K)):
        per_q = []
        for rho in range(ROWS_NA_Q):
            qr = r_first + rho
            row0 = min(max(qr - NA_KH // 2, 0), rows - NA_KH)
            per_k = []
            for kap in range(ROWS_NA_K):
                kr = r0 + kap
                per_k.append(slabs[:, kr - qr + NA_KH - 1] if row0 <= kr < row0 + NA_KH else masked)
            per_q.append(jnp.stack(per_k, axis=2))
        variants.append(jnp.stack(per_q, axis=1))
    table = jnp.stack(variants, axis=0)
    return table.reshape(3, NA_HEADS, ROWS_NA_Q * GRID_W, ROWS_NA_K * GRID_W)


def _na_lat_kernel(q_ref, k_ref, v_ref, kc_ref, vc_ref, bias_ref, o_ref):
    g = pl.program_id(1)
    rows = DEC_SEQ // GRID_W
    r0 = jnp.clip(ROWS_NA_Q * g - NA_KH // 2, 0, rows - ROWS_NA_K)
    start = pl.multiple_of(r0 * GRID_W, GRID_W)
    nk = ROWS_NA_K * GRID_W
    k = k_ref[pl.ds(start, nk), :].astype(BF)
    v = v_ref[pl.ds(start, nk), :].astype(BF)
    _na_heads(q_ref[...] * (NA_HD ** -0.5), k, v, o_ref, kc=kc_ref[0].astype(BF), vc=vc_ref[0].astype(BF),
              bias_ref=bias_ref)


def _na_lat(z, kc, vc, bias):
    tq = ROWS_NA_Q * GRID_W
    ng = DEC_SEQ // tq
    nk = ROWS_NA_K * GRID_W
    qb = T_CTX // tq
    zb = T_CTX // DEC_SEQ

    def variant(g):
        return jnp.where(g == 0, 0, jnp.where(g == ng - 1, 2, 1))

    return pl.pallas_call(
        _na_lat_kernel,
        out_shape=jax.ShapeDtypeStruct((T_LAT, 512), BF),
        grid=(DEC_BATCH, ng),
        in_specs=[pl.BlockSpec((tq, 512), lambda b, g: (qb + b * ng + g, COL_NAQ)),
                  pl.BlockSpec((DEC_SEQ, 512), lambda b, g: (zb + b, COL_NAK)),
                  pl.BlockSpec((DEC_SEQ, 512), lambda b, g: (zb + b, COL_NAV)),
                  pl.BlockSpec((1, PAST_LEN, 512), lambda b, g: (b, 0, 0)),
                  pl.BlockSpec((1, PAST_LEN, 512), lambda b, g: (b, 0, 0)),
                  pl.BlockSpec((1, NA_HEADS, tq, nk), lambda b, g: (variant(g), 0, 0, 0))],
        out_specs=pl.BlockSpec((tq, 512), lambda b, g: (b * ng + g, 0)),
        compiler_params=_params(2, VMEM_BIG),
        name="na_lat",
    )(z, z, z, kc, vc, bias)


def _s5_discretize(lam_re, lam_im, log_dt, b_re, b_im):
    dt = jnp.exp(log_dt.astype(F32))[:, None]
    lr = lam_re.astype(F32)
    li = lam_im.astype(F32)
    mag = jnp.exp(lr * dt)
    ar = mag * jnp.cos(li * dt)
    ai = mag * jnp.sin(li * dt)
    den = lr * lr + li * li
    fr = ((ar - 1.0) * lr + ai * li) / den
    fi = (ai * lr - (ar - 1.0) * li) / den
    br = b_re.astype(F32)
    bi = b_im.astype(F32)
    bbr = fr[..., None] * br - fi[..., None] * bi
    bbi = fr[..., None] * bi + fi[..., None] * br
    return ar, ai, bbr, bbi


def _s5_weights(ar, ai, bbr, bbi, c_re, c_im):
    eye = jnp.eye(8, dtype=F32)
    nblk = SSM_G // 8

    def in_w(bb):
        return jnp.einsum('jgnp,gh->jgphn', bb.reshape(nblk, 8, SSM_N, SSM_P), eye).reshape(nblk, LANES, 8 * SSM_N)

    def out_w(c):
        return jnp.einsum('jgpn,gh->jgnhp', c.astype(F32).reshape(nblk, 8, SSM_P, SSM_N), eye).reshape(
            nblk, 8 * SSM_N, LANES)

    wb = jnp.concatenate([in_w(bbr), in_w(bbi)], axis=-1).astype(BF)
    wc = jnp.concatenate([out_w(c_re), -out_w(c_im)], axis=1).astype(BF)
    a_row = jnp.concatenate([ar.reshape(1, SSM_S), ai.reshape(1, SSM_S)], axis=1)
    return wb, wc, a_row


def _s5_kernel(uf_ref, ub_ref, wbf_ref, wbb_ref, wcf_ref, wcb_ref, af_ref, ab_ref, h0f_ref, h0b_ref,
               yf_ref, yb_ref, hff_ref, hfb_ref, hf_scr, hb_scr, *, nc, tc):
    r = nc * tc
    nblk = SSM_G // 8
    sb = 8 * SSM_N

    @pl.when(pl.program_id(0) == 0)
    def _():
        hf_scr[0:nc, :] = h0f_ref[...]
        hb_scr[r:r + nc, :] = h0b_ref[...]

    for j in range(nblk):
        cs = slice(LANES * j, LANES * (j + 1))
        pf = jnp.dot(uf_ref[:, cs], wbf_ref[j], preferred_element_type=F32)
        hf_scr[nc:nc + r, sb * j:sb * (j + 1)] = pf[:, :sb]
        hf_scr[nc:nc + r, SSM_S + sb * j:SSM_S + sb * (j + 1)] = pf[:, sb:]
        pb = jnp.dot(ub_ref[:, cs], wbb_ref[j], preferred_element_type=F32)
        hb_scr[0:r, sb * j:sb * (j + 1)] = pb[:, :sb]
        hb_scr[0:r, SSM_S + sb * j:SSM_S + sb * (j + 1)] = pb[:, sb:]

    def step(t, carry):
        f_prev = pl.multiple_of(t * nc, nc)
        f_cur = pl.multiple_of((t + 1) * nc, nc)
        b_cur = pl.multiple_of((tc - 1 - t) * nc, nc)
        b_prev = pl.multiple_of((tc - t) * nc, nc)
        for scr, a_ref, prev, cur in ((hf_scr, af_ref, f_prev, f_cur), (hb_scr, ab_ref, b_prev, b_cur)):
            for lb in range(nblk):
                lr = slice(sb * lb, sb * (lb + 1))
                li = slice(SSM_S + sb * lb, SSM_S + sb * (lb + 1))
                ar, ai = a_ref[0:8, lr], a_ref[0:8, li]
                for g in range(nc // 8):
                    rp = pl.ds(pl.multiple_of(prev + 8 * g, 8), 8)
                    rc = pl.ds(pl.multiple_of(cur + 8 * g, 8), 8)
                    hr, hi = scr[rp, lr], scr[rp, li]
                    scr[rc, lr] = ar * hr - ai * hi + scr[rc, lr]
                    scr[rc, li] = ar * hi + ai * hr + scr[rc, li]
        return carry

    lax.fori_loop(0, tc, step, 0)

    for j in range(nblk):
        cs = slice(LANES * j, LANES * (j + 1))
        lr = slice(sb * j, sb * (j + 1))
        li = slice(SSM_S + sb * j, SSM_S + sb * (j + 1))
        yf_ref[:, cs] = (jnp.dot(hf_scr[nc:nc + r, lr].astype(BF), wcf_ref[j, :sb], preferred_element_type=F32)
                         + jnp.dot(hf_scr[nc:nc + r, li].astype(BF), wcf_ref[j, sb:], preferred_element_type=F32))
        yb_ref[:, cs] = (jnp.dot(hb_scr[0:r, lr].astype(BF), wcb_ref[j, :sb], preferred_element_type=F32)
                         + jnp.dot(hb_scr[0:r, li].astype(BF), wcb_ref[j, sb:], preferred_element_type=F32))

    last_f = hf_scr[r:r + nc, :]
    last_b = hb_scr[0:nc, :]
    hf_scr[0:nc, :] = last_f
    hb_scr[r:r + nc, :] = last_b
    hff_ref[...] = last_f
    hfb_ref[...] = last_b


def _s5_scan(u_tm, nc, wts_f, wts_b, h0f, h0b):
    rows = u_tm.shape[0]
    tc = S5_ROWS // nc
    n_chunks = rows // S5_ROWS
    wbf, wcf, af = wts_f
    wbb, wcb, ab = wts_b
    af = jnp.broadcast_to(af, (nc, 2 * SSM_S))
    ab = jnp.broadcast_to(ab, (nc, 2 * SSM_S))
    const3 = lambda c: (0, 0, 0)
    const2 = lambda c: (0, 0)
    return pl.pallas_call(
        functools.partial(_s5_kernel, nc=nc, tc=tc),
        out_shape=(jax.ShapeDtypeStruct((rows, SSM_CH), F32), jax.ShapeDtypeStruct((rows, SSM_CH), F32),
                   jax.ShapeDtypeStruct((nc, 2 * SSM_S), F32), jax.ShapeDtypeStruct((nc, 2 * SSM_S), F32)),
        grid=(n_chunks,),
        in_specs=[pl.BlockSpec((S5_ROWS, SSM_CH), lambda c: (c, 0)),
                  pl.BlockSpec((S5_ROWS, SSM_CH), lambda c: (n_chunks - 1 - c, 0)),
                  pl.BlockSpec(wbf.shape, const3), pl.BlockSpec(wbb.shape, const3),
                  pl.BlockSpec(wcf.shape, const3), pl.BlockSpec(wcb.shape, const3),
                  pl.BlockSpec((nc, 2 * SSM_S), const2), pl.BlockSpec((nc, 2 * SSM_S), const2),
                  pl.BlockSpec((nc, 2 * SSM_S), const2), pl.BlockSpec((nc, 2 * SSM_S), const2)],
        out_specs=(pl.BlockSpec((S5_ROWS, SSM_CH), lambda c: (c, 0)),
                   pl.BlockSpec((S5_ROWS, SSM_CH), lambda c: (n_chunks - 1 - c, 0)),
                   pl.BlockSpec((nc, 2 * SSM_S), const2), pl.BlockSpec((nc, 2 * SSM_S), const2)),
        scratch_shapes=[pltpu.VMEM((S5_ROWS + nc, 2 * SSM_S), F32), pltpu.VMEM((S5_ROWS + nc, 2 * SSM_S), F32)],
        compiler_params=_params(1, VMEM_BIG),
        name="s5_scan",
    )(u_tm, u_tm, wbf, wbb, wcf, wcb, af, ab, h0f, h0b)


def _ssm_glu_kernel(u_ref, y_ref, d_ref, w_ref, o_ref):
    y = jax.nn.gelu(u_ref[...] * d_ref[...] + y_ref[...])
    o_ref[...] = (y * jax.nn.sigmoid(jnp.dot(y.astype(BF), w_ref[...], preferred_element_type=F32))).astype(BF)


def _ssm_glu(z, y_scan, d, w_glu):
    tm = 1024
    return pl.pallas_call(
        _ssm_glu_kernel,
        out_shape=jax.ShapeDtypeStruct((T_ALL, SSM_CH), BF),
        grid=(T_ALL // tm,),
        in_specs=[pl.BlockSpec((tm, SSM_CH), lambda i: (i, COL_U)),
                  pl.BlockSpec((tm, SSM_CH), lambda i: (i, 0)),
                  pl.BlockSpec((1, SSM_CH), lambda i: (0, 0)),
                  pl.BlockSpec((SSM_CH, SSM_CH), lambda i: (0, 0))],
        out_specs=pl.BlockSpec((tm, SSM_CH), lambda i: (i, 0)),
        compiler_params=_params(1),
        name="ssm_glu",
    )(z, y_scan, d.reshape(1, SSM_CH), w_glu)


CONV_PAD = 16
CONV_ROWS = 64


def _conv_kernel(a_ref, g_ref, w_ref, b_ref, lg_ref, lb_ref, o_ref, z_scr, *, seq):
    z_scr[0:CONV_PAD, :] = jnp.zeros((CONV_PAD, CONV_CH), F32)
    z_scr[CONV_PAD + seq:2 * CONV_PAD + seq, :] = jnp.zeros((CONV_PAD, CONV_CH), F32)
    z_scr[CONV_PAD:CONV_PAD + seq, :] = a_ref[...] * jax.nn.sigmoid(g_ref[...])
    off = CONV_PAD - CONV_K // 2

    def tile(i, carry):
        base = pl.multiple_of(i * CONV_ROWS, CONV_ROWS)
        win = z_scr[pl.ds(base, CONV_ROWS + 2 * CONV_PAD), :]
        acc = jnp.zeros((CONV_ROWS, CONV_CH), F32)
        for k in range(CONV_K):
            acc = acc + win[off + k:off + k + CONV_ROWS, :] * w_ref[k:k + 1, :]
        acc = acc + b_ref[...]
        mu = jnp.mean(acc, axis=-1, keepdims=True)
        cen = acc - mu
        var = jnp.mean(cen * cen, axis=-1, keepdims=True)
        y = cen * lax.rsqrt(var + EPS) * lg_ref[...] + lb_ref[...]
        o_ref[pl.ds(base, CONV_ROWS), :] = _silu(y).astype(BF)
        return carry

    lax.fori_loop(0, seq // CONV_ROWS, tile, 0)


def _conv_module(z, seq, row_block0, n_seq, w, b, lg, lb):
    row = lambda v: v.reshape(1, CONV_CH)
    wpad = jnp.pad(w, ((0, 32 - CONV_K), (0, 0)))
    vec = pl.BlockSpec((1, CONV_CH), lambda s: (0, 0))
    return pl.pallas_call(
        functools.partial(_conv_kernel, seq=seq),
        out_shape=jax.ShapeDtypeStruct((n_seq * seq, CONV_CH), BF),
        grid=(n_seq,),
        in_specs=[pl.BlockSpec((seq, CONV_CH), lambda s: (row_block0 + s, COL_CA)),
                  pl.BlockSpec((seq, CONV_CH), lambda s: (row_block0 + s, COL_CG)),
                  pl.BlockSpec((32, CONV_CH), lambda s: (0, 0)), vec, vec, vec],
        out_specs=pl.BlockSpec((seq, CONV_CH), lambda s: (s, 0)),
        scratch_shapes=[pltpu.VMEM((seq + 2 * CONV_PAD, CONV_CH), F32)],
        compiler_params=_params(1, VMEM_BIG),
        name="conv_module",
    )(z, z, wpad, row(b), row(lg), row(lb))


def _merge_kernel(h_ref, o_ref, wg_ref, wb_ref, y_ref, acc_ref):
    j = pl.program_id(2)
    gate = jax.nn.sigmoid(jnp.dot(h_ref[...], wg_ref[...], preferred_element_type=F32))
    t = gate * jnp.dot(o_ref[0], wb_ref[0], preferred_element_type=F32)

    @pl.when(j == 0)
    def _():
        acc_ref[...] = t

    @pl.when(j > 0)
    def _():
        acc_ref[...] += t

    @pl.when(j == N_BRANCH - 1)
    def _():
        y_ref[...] = acc_ref[...].astype(BF)


def _merge(h, branches, w_gate, w_br):
    tm, tn = 512, 1024
    nct = D_MODEL // tn
    return pl.pallas_call(
        _merge_kernel,
        out_shape=jax.ShapeDtypeStruct((T_ALL, D_MODEL), BF),
        grid=(T_ALL // tm, nct, N_BRANCH),
        in_specs=[pl.BlockSpec((tm, D_MODEL), lambda i, c, j: (i, 0)),
                  pl.BlockSpec((1, tm, 512), lambda i, c, j: (j, i, 0)),
                  pl.BlockSpec((D_MODEL, tn), lambda i, c, j: (0, j * nct + c)),
                  pl.BlockSpec((1, 512, tn), lambda i, c, j: (j, 0, c))],
        out_specs=pl.BlockSpec((tm, tn), lambda i, c, j: (i, c)),
        scratch_shapes=[pltpu.VMEM((tm, tn), F32)],
        compiler_params=_params(3, VMEM_BIG),
        name="merge",
    )(h, branches, w_gate, w_br)


def _pack_halves(v):
    k = v.shape[1] // 2
    bits = lax.bitcast_convert_type(v.astype(BF).astype(F32), jnp.uint32)
    return (bits[:, :k] >> 16) | bits[:, k:]


def _unpack_halves(w):
    lo = lax.bitcast_convert_type(w << 16, F32)
    hi = lax.bitcast_convert_type(w & jnp.uint32(0xFFFF0000), F32)
    return lo, hi


def _to_tile_rows(v, ref):
    for s in range(8):
        ref[:, s, :] = v[:, LANES * s:LANES * (s + 1)]


def _from_tile_rows(ref):
    return jnp.concatenate([ref[:, s, :] for s in range(8)], axis=1)


def _out_router_kernel(y_ref, w_ref, x_ref, mod_ref, g_ref, rh_ref, rl_ref, rb_ref,
                       xo_ref, h_ref, hp_ref, idx_ref, wts_ref, rank_ref, cnt_ref):
    x = x_ref[...] + mod_ref[0, 2:3, :] * jnp.dot(y_ref[...], w_ref[...], preferred_element_type=F32)
    xo_ref[...] = x
    n = x * lax.rsqrt(jnp.mean(x * x, axis=-1, keepdims=True) + EPS) * g_ref[...]
    h = n * (1.0 + mod_ref[0, 4:5, :]) + mod_ref[0, 3:4, :]
    hb = h.astype(BF)
    h_ref[...] = hb
    _to_tile_rows(_pack_halves(h), hp_ref)
    h_lo = (h - hb.astype(F32)).astype(BF)
    logits = (jnp.dot(hb, rh_ref[...], preferred_element_type=F32)
              + jnp.dot(h_lo, rh_ref[...], preferred_element_type=F32)
              + jnp.dot(hb, rl_ref[...], preferred_element_type=F32))
    scores = jax.nn.sigmoid(logits)
    lane = lax.broadcasted_iota(jnp.int32, scores.shape, 1)
    cand = jnp.where(lane < N_EXPERTS, scores + rb_ref[...], -jnp.inf)
    idx = jnp.zeros(scores.shape, jnp.int32)
    wts = jnp.zeros(scores.shape, F32)
    total = jnp.zeros((scores.shape[0], 1), F32)
    member = jnp.zeros(scores.shape, F32)
    firsts = []
    for k in range(TOP_K):
        m = jnp.max(cand, axis=-1, keepdims=True)
        first = jnp.min(jnp.where(cand == m, lane, LANES), axis=-1, keepdims=True)
        hit = lane == first
        s_k = jnp.sum(jnp.where(hit, scores, 0.0), axis=-1, keepdims=True)
        idx = jnp.where(lane == k, first, idx)
        wts = jnp.where(lane == k, s_k, wts)
        total = total + s_k
        member = jnp.where(hit, 1.0, member)
        cand = jnp.where(hit, -jnp.inf, cand)
        firsts.append(first)
    idx_ref[...] = idx
    wts_ref[...] = wts / total * ROUTE_SCALE

    @pl.when(pl.program_id(0) == 0)
    def _():
        cnt_ref[...] = jnp.zeros(cnt_ref.shape, F32)

    tm = scores.shape[0]
    earlier = (lax.broadcasted_iota(jnp.int32, (tm, tm), 1) < lax.broadcasted_iota(jnp.int32, (tm, tm), 0))
    prior = jnp.dot(earlier.astype(BF), member.astype(BF), preferred_element_type=F32) + cnt_ref[0:1, :]
    rank = jnp.zeros(scores.shape, F32)
    for k in range(TOP_K):
        r_k = jnp.sum(jnp.where(lane == firsts[k], prior, 0.0), axis=-1, keepdims=True)
        rank = jnp.where(lane == k, r_k, rank)
    rank_ref[...] = rank.astype(jnp.int32)
    cnt_ref[...] = cnt_ref[...] + jnp.sum(member, axis=0, keepdims=True)


def _out_router(y, w_out, x, modt, g2, r_hi, r_lo, r_bias):
    tm = 512
    full = lambda i: (0, 0)
    rows = lambda width: pl.BlockSpec((tm, width), lambda i: (i, 0))
    return pl.pallas_call(
        _out_router_kernel,
        out_shape=(jax.ShapeDtypeStruct((T_ALL, D_MODEL), F32), jax.ShapeDtypeStruct((T_ALL, D_MODEL), BF),
                   jax.ShapeDtypeStruct((T_ALL, 8, LANES), jnp.uint32),
                   jax.ShapeDtypeStruct((T_ALL, LANES), jnp.int32), jax.ShapeDtypeStruct((T_ALL, LANES), F32),
                   jax.ShapeDtypeStruct((T_ALL, LANES), jnp.int32), jax.ShapeDtypeStruct((8, LANES), F32)),
        grid=(T_ALL // tm,),
        in_specs=[rows(D_MODEL),
                  pl.BlockSpec((D_MODEL, D_MODEL), full),
                  rows(D_MODEL),
                  pl.BlockSpec((1, 8, D_MODEL), lambda i: (i, 0, 0)),
                  pl.BlockSpec((1, D_MODEL), full),
                  pl.BlockSpec((D_MODEL, LANES), full), pl.BlockSpec((D_MODEL, LANES), full),
                  pl.BlockSpec((1, LANES), full)],
        out_specs=(rows(D_MODEL), rows(D_MODEL), pl.BlockSpec((tm, 8, LANES), lambda i: (i, 0, 0)),
                   rows(LANES), rows(LANES), rows(LANES), pl.BlockSpec((8, LANES), full)),
        compiler_params=_params(1, VMEM_BIG),
        name="out_router",
    )(y, w_out, x, modt, g2.reshape(1, D_MODEL), r_hi, r_lo, r_bias)


MOE_TILE = 256
MOE_PAIRS = T_ALL * TOP_K
MOE_SLOTS = -(-(MOE_PAIRS + N_EXPERTS * (MOE_TILE - 1)) // MOE_TILE) * MOE_TILE
MOE_NT = MOE_SLOTS // MOE_TILE
MOE_SPARE = MOE_SLOTS - MOE_PAIRS
COPY_CHUNK = 1024


def _route_plan(idx, rank, counts):
    i32 = jnp.int32
    experts = jnp.arange(N_EXPERTS, dtype=i32)
    cnt = counts[0, :N_EXPERTS].astype(i32)
    padded = (cnt + MOE_TILE - 1) // MOE_TILE * MOE_TILE
    upto = experts[:, None] >= experts[None, :]
    ends = jnp.sum(jnp.where(upto, padded[None, :], 0), axis=1)
    starts = ends - padded
    total = ends[-1]
    onehot = idx[:, :TOP_K, None] == experts
    slot = jnp.sum(jnp.where(onehot, starts, 0), axis=-1) + rank[:, :TOP_K]
    n_pad = padded - cnt
    pad_ends = jnp.sum(jnp.where(upto, n_pad[None, :], 0), axis=1)
    q = jnp.arange(MOE_SPARE, dtype=i32)
    group = jnp.sum((pad_ends[None, :] <= q[:, None]).astype(i32), axis=1)
    pad_base = starts + cnt - (pad_ends - n_pad)
    in_group = jnp.sum(jnp.where(group[:, None] == experts, pad_base, 0), axis=1) + q
    spare = jnp.where(group < N_EXPERTS, in_group, total + q - pad_ends[-1])
    dst = jnp.concatenate([slot.reshape(-1), spare])
    tile_start = jnp.arange(MOE_NT, dtype=i32) * MOE_TILE
    tile_e = jnp.sum((starts[None, :] <= tile_start[:, None]).astype(i32), axis=1) - 1
    tile_e = jnp.clip(tile_e, 0, N_EXPERTS - 1)
    tile_v = (tile_start < total).astype(i32)
    return dst, slot.T.reshape(-1), tile_e, tile_v


def _dispatch_kernel(dst_ref, h_ref, o_hbm, xb, sem, *, token_steps):
    i = pl.program_id(0)

    @pl.when(i < token_steps)
    def _():
        xb[...] = h_ref[...]

    @pl.when(i >= token_steps)
    def _():
        xb[...] = jnp.zeros(xb.shape, xb.dtype)

    def copy(t, d):
        return pltpu.make_async_copy(xb.at[t], o_hbm.at[d], sem.at[0])

    for r in range(COPY_CHUNK):
        copy(r // TOP_K, dst_ref[0, 0, r]).start()
    for r in range(COPY_CHUNK):
        copy(0, 0).wait()


def _dispatch(hp, dst):
    tok = COPY_CHUNK // TOP_K
    token_steps = T_ALL // tok
    steps = MOE_SLOTS // COPY_CHUNK
    return pl.pallas_call(
        functools.partial(_dispatch_kernel, token_steps=token_steps),
        out_shape=jax.ShapeDtypeStruct((MOE_SLOTS, 8, LANES), jnp.uint32),
        grid_spec=pltpu.PrefetchScalarGridSpec(
            num_scalar_prefetch=0, grid=(steps,),
            in_specs=[pl.BlockSpec((1, 1, COPY_CHUNK), lambda i: (i, 0, 0), memory_space=pltpu.SMEM),
                      pl.BlockSpec((tok, 8, LANES), lambda i: (jnp.minimum(i, token_steps - 1), 0, 0))],
            out_specs=pl.BlockSpec(memory_space=pl.ANY),
            scratch_shapes=[pltpu.VMEM((tok, 8, LANES), jnp.uint32), pltpu.SemaphoreType.DMA((1,))]),
        compiler_params=_params(1),
        name="moe_dispatch",
    )(dst.reshape(steps, 1, COPY_CHUNK), hp)


def _collect_kernel(src_ref, y_hbm, o_ref, sem):
    def copy(s, r):
        return pltpu.make_async_copy(y_hbm.at[s], o_ref.at[r], sem.at[0])

    for r in range(COPY_CHUNK):
        copy(src_ref[0, 0, r], r).start()
    for r in range(COPY_CHUNK):
        copy(0, r).wait()


def _collect(ys, src):
    steps = MOE_PAIRS // COPY_CHUNK
    return pl.pallas_call(
        _collect_kernel,
        out_shape=jax.ShapeDtypeStruct((MOE_PAIRS, 8, LANES), jnp.uint32),
        grid_spec=pltpu.PrefetchScalarGridSpec(
            num_scalar_prefetch=0, grid=(steps,),
            in_specs=[pl.BlockSpec((1, 1, COPY_CHUNK), lambda i: (i, 0, 0), memory_space=pltpu.SMEM),
                      pl.BlockSpec(memory_space=pl.ANY)],
            out_specs=pl.BlockSpec((COPY_CHUNK, 8, LANES), lambda i: (i, 0, 0)),
            scratch_shapes=[pltpu.SemaphoreType.DMA((1,))]),
        compiler_params=_params(1),
        name="moe_collect",
    )(src.reshape(steps, 1, COPY_CHUNK), ys)


def _gmm_kernel(te_ref, tv_ref, x_ref, w1_ref, w3_ref, w2_ref, y_ref, w13_scr, w2_scr):
    i = pl.program_id(0)

    @pl.when(tv_ref[i] == 1)
    def _():
        @pl.when(jnp.logical_or(i == 0, te_ref[i] != te_ref[jnp.maximum(i - 1, 0)]))
        def _():
            w13_scr[:, :EXPERT_F] = w1_ref[0, 0].astype(BF)
            w13_scr[:, EXPERT_F:] = w3_ref[0, 0].astype(BF)
            w2_scr[...] = w2_ref[0, 0].astype(BF)

        lo, hi = _unpack_halves(_from_tile_rows(x_ref))
        half = D_MODEL // 2
        ab = (jnp.dot(lo.astype(BF), w13_scr[:half, :], preferred_element_type=F32)
              + jnp.dot(hi.astype(BF), w13_scr[half:, :], preferred_element_type=F32))
        zed = (_silu(ab[:, :EXPERT_F]) * ab[:, EXPERT_F:]).astype(BF)
        _to_tile_rows(_pack_halves(jnp.dot(zed, w2_scr[...], preferred_element_type=F32)), y_ref)

    @pl.when(tv_ref[i] == 0)
    def _():
        y_ref[...] = jnp.zeros(y_ref.shape, y_ref.dtype)


def _grouped_experts(xs, tile_e, tile_v, w1, w3, w2, layer):
    wspec = lambda shape: pl.BlockSpec((1, 1) + shape, lambda i, te, tv: (layer, te[i], 0, 0))
    tile = pl.BlockSpec((MOE_TILE, 8, LANES), lambda i, te, tv: (i, 0, 0))
    grid_spec = pltpu.PrefetchScalarGridSpec(
        num_scalar_prefetch=2,
        grid=(MOE_NT,),
        in_specs=[tile, wspec((D_MODEL, EXPERT_F)), wspec((D_MODEL, EXPERT_F)), wspec((EXPERT_F, D_MODEL))],
        out_specs=tile,
        scratch_shapes=[pltpu.VMEM((D_MODEL, 2 * EXPERT_F), BF), pltpu.VMEM((EXPERT_F, D_MODEL), BF)])
    return pl.pallas_call(
        _gmm_kernel,
        out_shape=jax.ShapeDtypeStruct(xs.shape, jnp.uint32),
        grid_spec=grid_spec,
        compiler_params=_params(1, VMEM_BIG),
        name="grouped_experts",
    )(tile_e, tile_v, xs, w1, w3, w2)


def _combine_kernel(x_ref, h_ref, wts_ref, mod_ref, w1_ref, w3_ref, w2_ref, *rest):
    y_refs, o_ref = rest[:TOP_K], rest[TOP_K]
    h = h_ref[...]
    a = jnp.dot(h, w1_ref[0].astype(BF), preferred_element_type=F32)
    b = jnp.dot(h, w3_ref[0].astype(BF), preferred_element_type=F32)
    acc = jnp.dot((_silu(a) * b).astype(BF), w2_ref[0].astype(BF), preferred_element_type=F32)
    half = D_MODEL // 2
    acc_lo, acc_hi = acc[:, :half], acc[:, half:]
    wts = wts_ref[...]
    for k in range(TOP_K):
        lo, hi = _unpack_halves(_from_tile_rows(y_refs[k]))
        w_k = wts[:, k:k + 1]
        acc_lo = acc_lo + w_k * lo
        acc_hi = acc_hi + w_k * hi
    gate = mod_ref[0, 5:6, :]
    o_ref[:, :half] = x_ref[:, :half] + gate[:, :half] * acc_lo
    o_ref[:, half:] = x_ref[:, half:] + gate[:, half:] * acc_hi


def _combine(x, h, wts, modt, sw1, sw3, sw2, y, layer):
    tm = 256
    nt = T_ALL // tm
    rows = lambda width: pl.BlockSpec((tm, width), lambda i: (i, 0))
    y_specs = [pl.BlockSpec((tm, 8, LANES), functools.partial(lambda i, k: (k * nt + i, 0, 0), k=k))
               for k in range(TOP_K)]
    return pl.pallas_call(
        _combine_kernel,
        out_shape=jax.ShapeDtypeStruct((T_ALL, D_MODEL), F32),
        grid=(nt,),
        in_specs=[rows(D_MODEL), rows(D_MODEL), rows(LANES),
                  pl.BlockSpec((1, 8, D_MODEL), lambda i: (i, 0, 0)),
                  pl.BlockSpec((1, D_MODEL, EXPERT_F), lambda i: (layer, 0, 0)),
                  pl.BlockSpec((1, D_MODEL, EXPERT_F), lambda i: (layer, 0, 0)),
                  pl.BlockSpec((1, EXPERT_F, D_MODEL), lambda i: (layer, 0, 0))] + y_specs,
        out_specs=rows(D_MODEL),
        compiler_params=_params(1, VMEM_BIG),
        name="moe_combine",
    )(x, h, wts, modt, sw1, sw3, sw2, *([y] * TOP_K))


def _final_kernel(x_ref, g_ref, o_ref):
    x = x_ref[...]
    o_ref[...] = x * lax.rsqrt(jnp.mean(x * x, axis=-1, keepdims=True) + EPS) * g_ref[...]


def _final_norm(x, g):
    tm = 512
    return pl.pallas_call(
        _final_kernel,
        out_shape=jax.ShapeDtypeStruct((T_ALL, D_MODEL), F32),
        grid=(T_ALL // tm,),
        in_specs=[pl.BlockSpec((tm, D_MODEL), lambda i: (i, 0)), pl.BlockSpec((1, D_MODEL), lambda i: (0, 0))],
        out_specs=pl.BlockSpec((tm, D_MODEL), lambda i: (i, 0)),
        compiler_params=_params(1),
        name="final_norm",
    )(x, g.reshape(1, D_MODEL))


def _time_major(u, n_chain):
    b, l, c = u.shape
    u = jnp.transpose(u, (1, 0, 2))
    u = jnp.pad(u, ((0, 0), (0, n_chain - b), (0, 0)))
    return u.reshape(l * n_chain, c)


def _from_time_major(y, b, l, n_chain):
    return jnp.transpose(y.reshape(l, n_chain, -1)[:, :b], (1, 0, 2)).reshape(b * l, -1)


def _diff_lambda(lam_q, lam_k, lam_init):
    lq = lam_q.astype(F32)
    lk = lam_k.astype(F32)
    return jnp.exp(jnp.sum(lq[0] * lk[0])) - jnp.exp(jnp.sum(lq[1] * lk[1])) + lam_init


def kernel(x_prompt, x_sample, cache_na_k, cache_na_v, cache_da_k, cache_da_v, state_ssm, c, c_ctx, ada_w, ada_b, norm1_g, norm2_g, w_in, w_gate, na_rpb, da_lam_q, da_lam_k, da_subln_g, ssm_lam_re, ssm_lam_im, ssm_log_dt, ssm_b_re, ssm_b_im, ssm_c_re, ssm_c_im, ssm_d, ssm_w_glu, conv_w, conv_b, conv_ln_g, conv_ln_b, w_br_na, w_br_da, w_br_ssm, w_br_conv, w_out, moe_router, moe_bias, moe_w1, moe_w3, moe_w2, shared_w1, shared_w3, shared_w2, final_g):
    x = jnp.concatenate([x_prompt.reshape(T_CTX, D_MODEL), x_sample.reshape(T_LAT, D_MODEL)], axis=0)
    cvec = jnp.zeros((8, D_MODEL), F32).at[0].set(c_ctx).at[1:1 + DEC_BATCH].set(c)
    mods_all = _adaln(cvec, ada_w, ada_b)
    cos, sin = _rope_tables()
    lat_chains = 8

    na_ks, na_vs, da_ks, da_vs, ssm_states = [], [], [], [], []
    for layer in range(DEPTH):
        mods = mods_all[layer, :1 + DEC_BATCH].reshape(1 + DEC_BATCH, 6, D_MODEL)
        modt = _mod_table(mods, 512)
        h1, z, u = _in_proj(x, modt, norm1_g[layer], w_in[layer].astype(BF))

        zc = z[:T_CTX]
        na_ks.append(zc[:, 512 * COL_NAK:512 * (COL_NAK + 1)].reshape(BATCH, SEQ, NA_HEADS, NA_HD))
        na_vs.append(zc[:, 512 * COL_NAV:512 * (COL_NAV + 1)].reshape(BATCH, SEQ, NA_HEADS, NA_HD))
        da_ks.append(zc[:, 512 * COL_DAK:512 * (COL_DAK + 1)].reshape(BATCH, SEQ, DA_HEADS, 2, DA_QK))
        da_vs.append(zc[:, 512 * COL_DAV:512 * (COL_DAV + 1)].reshape(BATCH, SEQ, DA_HEADS, DA_VD))

        lam_init = 0.8 - 0.6 * math.exp(-0.3 * layer)
        lam = _diff_lambda(da_lam_q[layer], da_lam_k[layer], lam_init).reshape(1, 1)
        subg = da_subln_g[layer].reshape(1, DA_VD)
        ona_c, oda_c = _attn_ctx(z, lam, subg, 1.0 - lam_init)
        ona_l = _na_lat(z, cache_na_k[:, layer].reshape(DEC_BATCH, PAST_LEN, 512),
                        cache_na_v[:, layer].reshape(DEC_BATCH, PAST_LEN, 512), _na_bias_table(na_rpb[layer]))
        qr, kr = _rope_lat(z, cos, sin)
        oda_l = _da_lat(qr, kr, z, cache_da_k[:, layer].reshape(DEC_BATCH, PAST_LEN, 512),
                        cache_da_v[:, layer].reshape(DEC_BATCH, PAST_LEN, 512), lam, subg, 1.0 - lam_init)

        wts = []
        for d in range(2):
            ar, ai, bbr, bbi = _s5_discretize(ssm_lam_re[layer, d], ssm_lam_im[layer, d], ssm_log_dt[layer, d],
                                              ssm_b_re[layer, d], ssm_b_im[layer, d])
            wts.append(_s5_weights(ar, ai, bbr, bbi, ssm_c_re[layer, d], ssm_c_im[layer, d]))
        zeros_c = jnp.zeros((BATCH, 2 * SSM_S), F32)
        yf, yb, hf, hb = _s5_scan(_time_major(u[:T_CTX].reshape(BATCH, SEQ, SSM_CH), BATCH), BATCH,
                                  wts[0], wts[1], zeros_c, zeros_c)
        y_ctx = _from_time_major(yf + yb, BATCH, SEQ, BATCH)
        ssm_states.append(jnp.stack([hf.reshape(BATCH, 2, SSM_G, SSM_N), hb.reshape(BATCH, 2, SSM_G, SSM_N)], axis=1))
        h0 = state_ssm[:, layer].astype(F32).reshape(DEC_BATCH, 2, 2 * SSM_S)
        h0 = jnp.pad(h0, ((0, lat_chains - DEC_BATCH), (0, 0), (0, 0)))
        yf, yb, _, _ = _s5_scan(_time_major(u[T_CTX:].reshape(DEC_BATCH, DEC_SEQ, SSM_CH), lat_chains), lat_chains,
                                wts[0], wts[1], h0[:, 0], h0[:, 1])
        y_lat = _from_time_major(yf + yb, DEC_BATCH, DEC_SEQ, lat_chains)
        o_ssm = _ssm_glu(z, jnp.concatenate([y_ctx, y_lat], axis=0), ssm_d[layer], ssm_w_glu[layer].astype(BF))

        conv_args = (conv_w[layer], conv_b[layer], conv_ln_g[layer], conv_ln_b[layer])
        o_conv = jnp.concatenate([_conv_module(z, SEQ, 0, BATCH, *conv_args),
                                  _conv_module(z, DEC_SEQ, T_CTX // DEC_SEQ, DEC_BATCH, *conv_args)], axis=0)

        branches = jnp.stack([jnp.concatenate([ona_c, ona_l], axis=0), jnp.concatenate([oda_c, oda_l], axis=0),
                              o_ssm, o_conv], axis=0)
        w_br = jnp.stack([w_br_na[layer], w_br_da[layer], w_br_ssm[layer], w_br_conv[layer]], axis=0).astype(BF)
        y = _merge(h1, branches, w_gate[layer].astype(BF), w_br)

        router = jnp.pad(moe_router[layer], ((0, 0), (0, LANES - N_EXPERTS)))
        r_hi = router.astype(BF)
        r_lo = (router - r_hi.astype(F32)).astype(BF)
        r_bias = jnp.pad(moe_bias[layer].astype(F32), (0, LANES - N_EXPERTS)).reshape(1, LANES)
        x_mid, h2, h2p, idx, wts, rank, counts = _out_router(y, w_out[layer].astype(BF), x, modt, norm2_g[layer],
                                                             r_hi, r_lo, r_bias)
        dst, slot, tile_e, tile_v = _route_plan(idx, rank, counts)
        xs = _dispatch(h2p, dst)
        ys = _grouped_experts(xs, tile_e, tile_v, moe_w1, moe_w3, moe_w2, layer)
        y_pairs = _collect(ys, slot)
        x = _combine(x_mid, h2, wts, _mod_table(mods, 256), shared_w1, shared_w3, shared_w2, y_pairs, layer)

    y = _final_norm(x, final_g)
    y_prompt = y[:T_CTX].reshape(BATCH, SEQ, D_MODEL)
    y_sample = y[T_CTX:].reshape(DEC_BATCH, DEC_SEQ, D_MODEL)
    return (y_prompt, y_sample, jnp.stack(na_ks, axis=1), jnp.stack(na_vs, axis=1), jnp.stack(da_ks, axis=1),
            jnp.stack(da_vs, axis=1), jnp.stack(ssm_states, axis=1))
```
